```python
import math
import jax, jax.numpy as jnp
from jax import lax
import numpy as np

D_MODEL = 1024
BATCH = 8
SEQ = 4096
DEPTH = 4

N_EVEN = (DEPTH + 1) // 2
N_ODD = DEPTH // 2
NORM_EPS = 1e-6

LRU_WIDTH = D_MODEL
LRU_HEADS = 8
LRU_HEAD_DIM = LRU_WIDTH // LRU_HEADS
LRU_CONV = 4
LRU_C = 8.0

SSD_INNER = D_MODEL
SSD_HEAD_DIM = 64
SSD_HEADS = SSD_INNER // SSD_HEAD_DIM
SSD_GROUPS = 2
SSD_HPG = SSD_HEADS // SSD_GROUPS
SSD_STATE = 128
SSD_CONV = 4
SSD_CHUNK = 128
SSD_CONV_DIM = SSD_INNER + 2 * SSD_GROUPS * SSD_STATE

EVEN_IN = 2 * LRU_WIDTH + SSD_INNER + SSD_CONV_DIM + SSD_HEADS
EVEN_MIX = LRU_WIDTH + SSD_INNER

SC_WIDTH = D_MODEL
SC_CONV = 3

PEER_HEADS = 8
PEER_NKEYS = 128
PEER_NEXPERTS = PEER_NKEYS * PEER_NKEYS
PEER_TOPK = 16
PEER_QDIM = 256
PEER_HALF = PEER_QDIM // 2
PEER_BLOCK = 128

kernel_name = "hybrid_rglru_ssd_shortconv_peer"


def rmsnorm(x, g):
    x32 = x.astype(jnp.float32)
    y = x32 * lax.rsqrt(jnp.mean(x32 * x32, axis=-1, keepdims=True) + NORM_EPS)
    return (y * g.astype(jnp.float32)).astype(x.dtype)


def causal_dwconv(x, w):
    k_width, ch = w.shape
    return lax.conv_general_dilated(
        x, w[:, None, :].astype(x.dtype), window_strides=(1,),
        padding=[(k_width - 1, 0)], dimension_numbers=('NWC', 'WIO', 'NWC'),
        feature_group_count=ch)


def rg_lru(x, gate_a_w, gate_a_b, gate_x_w, gate_x_b, lam):
    bsz, s, w = x.shape
    xh = x.reshape(bsz, s, LRU_HEADS, LRU_HEAD_DIM)
    r = jax.nn.sigmoid(jnp.einsum('bshi,hij->bshj', xh, gate_a_w).reshape(bsz, s, w) + gate_a_b)
    i = jax.nn.sigmoid(jnp.einsum('bshi,hij->bshj', xh, gate_x_w).reshape(bsz, s, w) + gate_x_b)
    log_a = -LRU_C * r.astype(jnp.float32) * jax.nn.softplus(-lam.astype(jnp.float32))
    a = jnp.exp(log_a)
    mult = jnp.sqrt(jnp.maximum(-jnp.expm1(2.0 * log_a), 0.0))
    b = mult * (i * x).astype(jnp.float32)

    def combine(left, right):
        a1, b1 = left
        a2, b2 = right
        return a1 * a2, a2 * b1 + b2

    _, h = lax.associative_scan(combine, (a, b), axis=1)
    return h.astype(x.dtype)


def ssd_scan(x, dt, a, bmat, cmat):
    bsz, s = x.shape[:2]
    nc, L = s // SSD_CHUNK, SSD_CHUNK
    x = x.reshape(bsz, nc, L, SSD_GROUPS, SSD_HPG, SSD_HEAD_DIM)
    dt = dt.reshape(bsz, nc, L, SSD_GROUPS, SSD_HPG)
    bm = bmat.reshape(bsz, nc, L, SSD_GROUPS, SSD_STATE)
    cm = cmat.reshape(bsz, nc, L, SSD_GROUPS, SSD_STATE)
    xdt = x * dt[..., None]
    da_cum = jnp.cumsum(dt * a.reshape(SSD_GROUPS, SSD_HPG), axis=2)

    seg = da_cum[:, :, :, None] - da_cum[:, :, None]
    causal = jnp.tril(jnp.ones((L, L), dtype=bool))[None, None, :, :, None, None]
    decay = jnp.exp(jnp.where(causal, seg, -jnp.inf))
    cb = jnp.einsum('bclgn,bcsgn->bclsg', cm, bm)
    y_diag = jnp.einsum('bclsgj,bcsgjp->bclgjp', cb[..., None] * decay, xdt)

    decay_to_end = jnp.exp(da_cum[:, :, -1:] - da_cum)
    states = jnp.einsum('bclgn,bclgjp->bcgjpn', bm, decay_to_end[..., None] * xdt)
    chunk_decay = jnp.exp(da_cum[:, :, -1])

    def step(h, inp):
        dec, st = inp
        return h * dec[..., None, None] + st, h

    h0 = jnp.zeros((bsz, SSD_GROUPS, SSD_HPG, SSD_HEAD_DIM, SSD_STATE), jnp.float32)
    _, prev = lax.scan(step, h0, (jnp.moveaxis(chunk_decay, 1, 0), jnp.moveaxis(states, 1, 0)))
    prev = jnp.moveaxis(prev, 0, 1)
    y_off = jnp.einsum('bclgn,bcgjpn->bclgjp', cm, prev) * jnp.exp(da_cum)[..., None]
    return (y_diag + y_off).reshape(bsz, s, SSD_HEADS, SSD_HEAD_DIM)


def even_mixer(h, w_in, lru_conv_w, lru_conv_b, lru_ga_w, lru_ga_b, lru_gx_w, lru_gx_b,
               lru_lam, ssd_conv_w, ssd_conv_b, ssd_dt_bias, ssd_a_log, ssd_d, ssd_norm_g, w_out):
    bsz, s, _ = h.shape
    proj = h @ w_in
    o1 = LRU_WIDTH
    o2 = o1 + LRU_WIDTH
    o3 = o2 + SSD_INNER
    o4 = o3 + SSD_CONV_DIM
    lru_gate, lru_x, ssd_z, ssd_xbc, ssd_dt = jnp.split(proj, [o1, o2, o3, o4], axis=-1)

    xa = causal_dwconv(lru_x, lru_conv_w) + lru_conv_b
    ya = jax.nn.gelu(lru_gate) * rg_lru(xa, lru_ga_w, lru_ga_b, lru_gx_w, lru_gx_b, lru_lam)

    xbc = jax.nn.silu(causal_dwconv(ssd_xbc, ssd_conv_w) + ssd_conv_b)
    xs, bs, cs = jnp.split(xbc, [SSD_INNER, SSD_INNER + SSD_GROUPS * SSD_STATE], axis=-1)
    xs32 = xs.astype(jnp.float32).reshape(bsz, s, SSD_HEADS, SSD_HEAD_DIM)
    dt = jax.nn.softplus(ssd_dt.astype(jnp.float32) + ssd_dt_bias.astype(jnp.float32))
    a = -jnp.exp(ssd_a_log.astype(jnp.float32))
    y = ssd_scan(xs32, dt, a,
                 bs.astype(jnp.float32).reshape(bsz, s, SSD_GROUPS, SSD_STATE),
                 cs.astype(jnp.float32).reshape(bsz, s, SSD_GROUPS, SSD_STATE))
    y = y + ssd_d.astype(jnp.float32)[:, None] * xs32
    y = y.reshape(bsz, s, SSD_INNER).astype(h.dtype)
    yb = rmsnorm(y * jax.nn.silu(ssd_z), ssd_norm_g)

    return jnp.concatenate([ya, yb], axis=-1) @ w_out


def odd_mixer(h, w_in, conv_w, w_out):
    b_gate, c_gate, v = jnp.split(h @ w_in, 3, axis=-1)
    return (b_gate * causal_dwconv(c_gate * v, conv_w)) @ w_out


def peer_ffn(h, w_query, sub_keys, expert_u, expert_v):
    bsz, s, d = h.shape
    tokens = h.reshape(-1, PEER_BLOCK, d)

    def block(xt):
        t = xt.shape[0]
        q = (xt @ w_query).reshape(t, PEER_HEADS, 2, PEER_HALF)
        sc = jnp.einsum('thpd,hpnd->thpn', q, sub_keys).astype(jnp.float32)
        sv, si = lax.top_k(sc, PEER_TOPK)
        cand = sv[:, :, 0, :, None] + sv[:, :, 1, None, :]
        cand_idx = si[:, :, 0, :, None] * PEER_NKEYS + si[:, :, 1, None, :]
        top_v, top_pos = lax.top_k(cand.reshape(t, PEER_HEADS, PEER_TOPK * PEER_TOPK), PEER_TOPK)
        idx = jnp.take_along_axis(cand_idx.reshape(t, PEER_HEADS, PEER_TOPK * PEER_TOPK), top_pos, axis=-1)
        g = jax.nn.softmax(top_v, axis=-1)
        act = jax.nn.gelu(jnp.einsum('thkd,td->thk', expert_u[idx], xt).astype(jnp.float32))
        wgt = (g * act).astype(xt.dtype)
        return jnp.einsum('thk,thkd->td', wgt, expert_v[idx])

    return lax.map(block, tokens).reshape(bsz, s, d)


def setup_inputs(seed: int = 0) -> dict:
    key = jax.random.key(seed)
    ks = iter(jax.random.split(key, 40))

    def nrm(shape, scale):
        return scale * jax.random.normal(next(ks), shape, jnp.float32)

    def gain(shape):
        return 1.0 + nrm(shape, 0.02)

    E, O = N_EVEN, N_ODD
    x = nrm((BATCH, SEQ, D_MODEL), 1.0)

    u = jax.random.uniform(next(ks), (E, LRU_WIDTH), jnp.float32, 0.9, 0.999)
    a_base = u ** (1.0 / LRU_C)
    lru_lambda = jnp.log(a_base) - jnp.log1p(-a_base)
    dt0 = jnp.exp(jax.random.uniform(next(ks), (E, SSD_HEADS), jnp.float32,
                                     math.log(1e-3), math.log(0.1)))
    ssd_dt_bias = dt0 + jnp.log(-jnp.expm1(-dt0))
    ssd_a_log = jnp.log(jax.random.uniform(next(ks), (E, SSD_HEADS), jnp.float32, 1.0, 16.0))

    return {
        "x": x,
        "even_norm_g": gain((E, D_MODEL)),
        "even_w_in": nrm((E, D_MODEL, EVEN_IN), D_MODEL ** -0.5),
        "lru_conv_w": nrm((E, LRU_CONV, LRU_WIDTH), LRU_CONV ** -0.5),
        "lru_conv_b": nrm((E, LRU_WIDTH), 0.01),
        "lru_gate_a_w": nrm((E, LRU_HEADS, LRU_HEAD_DIM, LRU_HEAD_DIM), LRU_HEAD_DIM ** -0.5),
        "lru_gate_a_b": nrm((E, LRU_WIDTH), 0.01),
        "lru_gate_x_w": nrm((E, LRU_HEADS, LRU_HEAD_DIM, LRU_HEAD_DIM), LRU_HEAD_DIM ** -0.5),
        "lru_gate_x_b": nrm((E, LRU_WIDTH), 0.01),
        "lru_lambda": lru_lambda,
        "ssd_conv_w": nrm((E, SSD_CONV, SSD_CONV_DIM), SSD_CONV ** -0.5),
        "ssd_conv_b": nrm((E, SSD_CONV_DIM), 0.01),
        "ssd_dt_bias": ssd_dt_bias,
        "ssd_a_log": ssd_a_log,
        "ssd_d": gain((E, SSD_HEADS)),
        "ssd_norm_g": gain((E, SSD_INNER)),
        "even_w_out": nrm((E, EVEN_MIX, D_MODEL), EVEN_MIX ** -0.5),
        "odd_norm_g": gain((O, D_MODEL)),
        "odd_w_in": nrm((O, D_MODEL, 3 * SC_WIDTH), D_MODEL ** -0.5),
        "odd_conv_w": nrm((O, SC_CONV, SC_WIDTH), SC_CONV ** -0.5),
        "odd_w_out": nrm((O, SC_WIDTH, D_MODEL), SC_WIDTH ** -0.5),
        "ffn_norm_g": gain((DEPTH, D_MODEL)),
        "peer_w_query": nrm((DEPTH, D_MODEL, PEER_HEADS * PEER_QDIM), D_MODEL ** -0.5),
        "peer_sub_keys": nrm((DEPTH, PEER_HEADS, 2, PEER_NKEYS, PEER_HALF), PEER_HALF ** -0.5),
        "peer_u": nrm((DEPTH, PEER_NEXPERTS, D_MODEL), D_MODEL ** -0.5),
        "peer_v": nrm((DEPTH, PEER_NEXPERTS, D_MODEL), (PEER_HEADS * PEER_TOPK) ** -0.5),
        "final_norm_g": gain((D_MODEL,)),
    }


def reference(x, even_norm_g, even_w_in, lru_conv_w, lru_conv_b, lru_gate_a_w, lru_gate_a_b,
              lru_gate_x_w, lru_gate_x_b, lru_lambda, ssd_conv_w, ssd_conv_b, ssd_dt_bias,
              ssd_a_log, ssd_d, ssd_norm_g, even_w_out, odd_norm_g, odd_w_in, odd_conv_w,
              odd_w_out, ffn_norm_g, peer_w_query, peer_sub_keys, peer_u, peer_v, final_norm_g):
    h = x
    for layer in range(DEPTH):
        i = layer // 2
        if layer % 2 == 0:
            h = h + even_mixer(rmsnorm(h, even_norm_g[i]), even_w_in[i], lru_conv_w[i], lru_conv_b[i],
                               lru_gate_a_w[i], lru_gate_a_b[i], lru_gate_x_w[i], lru_gate_x_b[i],
                               lru_lambda[i], ssd_conv_w[i], ssd_conv_b[i], ssd_dt_bias[i],
                               ssd_a_log[i], ssd_d[i], ssd_norm_g[i], even_w_out[i])
        else:
            h = h + odd_mixer(rmsnorm(h, odd_norm_g[i]), odd_w_in[i], odd_conv_w[i], odd_w_out[i])
        h = h + peer_ffn(rmsnorm(h, ffn_norm_g[layer]), peer_w_query[layer], peer_sub_keys[layer],
                         peer_u[layer], peer_v[layer])
    return rmsnorm(h, final_norm_g)
```

```python
import functools

import jax
import jax.numpy as jnp
from jax import lax
from jax.experimental import pallas as pl
from jax.experimental.pallas import tpu as pltpu

F32 = jnp.float32
BF16 = jnp.bfloat16

D_MODEL = 1024
NORM_EPS = 1e-6

LRU_WIDTH = 1024
LRU_HEADS = 8
LRU_HEAD_DIM = 128
LRU_C = 8.0

SSD_INNER = 1024
SSD_HEAD_DIM = 64
SSD_HEADS = 16
SSD_GROUPS = 2
SSD_HPG = 8
SSD_STATE = 128
SSD_CHUNK = 128
SSD_CONV_DIM = SSD_INNER + 2 * SSD_GROUPS * SSD_STATE

EVEN_MAIN = 2 * LRU_WIDTH + SSD_INNER + SSD_CONV_DIM
EVEN_PAD = EVEN_MAIN + 128

PEER_HEADS = 8
PEER_NKEYS = 128
PEER_NEXPERTS = PEER_NKEYS * PEER_NKEYS
PEER_TOPK = 16
PEER_HALF = 128

LANE = 128
SUBLANE = 8
VMEM_LIMIT = 56 * 1024 * 1024

NOT_TOP = 100.0

_CELLS = [(i, j) for i in range(PEER_TOPK) for j in range(PEER_TOPK) if (i + 1) * (j + 1) <= PEER_TOPK]


def _cparams(sem):
    return pltpu.CompilerParams(dimension_semantics=sem, vmem_limit_bytes=VMEM_LIMIT)


def _split_bf16(x, terms):
    parts = []
    rem = x
    for _ in range(terms):
        p = rem.astype(BF16)
        parts.append(p)
        rem = rem - p.astype(F32)
    return parts


def _dot_exact_rhs01(x, m01, terms):
    acc = None
    for p in _split_bf16(x, terms):
        d = jnp.dot(p, m01, preferred_element_type=F32)
        acc = d if acc is None else acc + d
    return acc


def _dot_exact_lhs01(m01, x, terms):
    acc = None
    for p in _split_bf16(x, terms):
        d = jnp.dot(m01, p, preferred_element_type=F32)
        acc = d if acc is None else acc + d
    return acc


def _softplus(x):
    return jnp.maximum(x, 0.0) + jnp.log1p(jnp.exp(-jnp.abs(x)))


def _expm1(x):
    u = jnp.exp(x)
    um1 = u - 1.0
    return jnp.where(um1 == 0.0, x, um1 * x / jnp.log(jnp.where(um1 == 0.0, 2.0, u)))


def _sigmoid(x):
    return 1.0 / (1.0 + jnp.exp(-x))


def _silu(x):
    return x * _sigmoid(x)


def _gelu_tanh(x):
    c = 0.7978845608028654
    return 0.5 * x * (1.0 + jnp.tanh(c * (x + 0.044715 * (x * x * x))))


def _nmm_body(x_ref, g_ref, w_ref, o_ref, *xn_out):
    x = x_ref[...]
    ms = jnp.mean(x * x, axis=-1, keepdims=True)
    xn = (x * lax.rsqrt(ms + NORM_EPS) * g_ref[...]).astype(BF16)
    o_ref[...] = jnp.dot(xn, w_ref[...], preferred_element_type=F32).astype(o_ref.dtype)
    if xn_out:
        xn_out[0][...] = xn


def _norm_matmul(x, g, w_bf16, out_dtype, tm, want_xn=False):
    t, d = x.shape
    n = w_bf16.shape[1]
    out_shape = [jax.ShapeDtypeStruct((t, n), out_dtype)]
    out_specs = [pl.BlockSpec((tm, n), lambda i: (i, 0))]
    if want_xn:
        out_shape.append(jax.ShapeDtypeStruct((t, d), BF16))
        out_specs.append(pl.BlockSpec((tm, d), lambda i: (i, 0)))
    res = pl.pallas_call(
        _nmm_body,
        grid=(t // tm,),
        in_specs=[pl.BlockSpec((tm, d), lambda i: (i, 0)),
                  pl.BlockSpec((1, d), lambda i: (0, 0)),
                  pl.BlockSpec((d, n), lambda i: (0, 0))],
        out_specs=out_specs,
        out_shape=out_shape,
        compiler_params=_cparams(("parallel",)),
        name="norm_matmul",
    )(x, g.reshape(1, d), w_bf16)
    return res if want_xn else res[0]


def _lru_body(gate_ref, x_ref, cw_ref, cb_ref, gaw_ref, gab_ref, gxw_ref, gxb_ref, lam_ref,
              o_ref, xpad_ref, a_ref, b_ref, carry_ref, *, ts):
    w = LRU_WIDTH

    @pl.when(pl.program_id(1) == 0)
    def _():
        xpad_ref[0:SUBLANE, :] = jnp.zeros((SUBLANE, w), F32)
        carry_ref[...] = jnp.zeros((SUBLANE, w), F32)

    xpad_ref[SUBLANE:SUBLANE + ts, :] = x_ref[...]
    xa = cb_ref[...] + cw_ref[3:4, :] * x_ref[...]
    for k in range(3):
        xa = xa + cw_ref[k:k + 1, :] * xpad_ref[5 + k:5 + k + ts, :]
    xpad_ref[0:SUBLANE, :] = x_ref[ts - SUBLANE:ts, :]

    sp = _softplus(-lam_ref[...])
    for hd in range(LRU_HEADS):
        sl = slice(hd * LRU_HEAD_DIM, (hd + 1) * LRU_HEAD_DIM)
        xh = xa[:, sl]
        xh16 = xh.astype(BF16)
        r = _sigmoid(jnp.dot(xh16, gaw_ref[hd], preferred_element_type=F32) + gab_ref[:, sl])
        ig = _sigmoid(jnp.dot(xh16, gxw_ref[hd], preferred_element_type=F32) + gxb_ref[:, sl])
        log_a = (-LRU_C) * r * sp[:, sl]
        a_ref[:, sl] = jnp.exp(log_a)
        mult = jnp.sqrt(jnp.maximum(-_expm1(2.0 * log_a), 0.0))
        b_ref[:, sl] = mult * (ig * xh)

    row = lax.broadcasted_iota(jnp.int32, (SUBLANE, w), 0)

    def step(i, carry):
        r0 = pl.multiple_of(i * SUBLANE, SUBLANE)
        a = a_ref[pl.ds(r0, SUBLANE), :]
        b = b_ref[pl.ds(r0, SUBLANE), :]
        for d in (1, 2, 4):
            keep = row >= d
            a_sh = jnp.where(keep, pltpu.roll(a, d, 0), 1.0)
            b_sh = jnp.where(keep, pltpu.roll(b, d, 0), 0.0)
            b = a * b_sh + b
            a = a * a_sh
        hcur = b + a * carry
        b_ref[pl.ds(r0, SUBLANE), :] = hcur
        return jnp.broadcast_to(hcur[SUBLANE - 1:SUBLANE, :], (SUBLANE, w))

    carry = lax.fori_loop(0, ts // SUBLANE, step, carry_ref[...], unroll=2)
    carry_ref[...] = carry
    o_ref[...] = (_gelu_tanh(gate_ref[...]) * b_ref[...]).astype(o_ref.dtype)


def _lru_branch(proj, bsz, seq, cw, cb, gaw, gab, gxw, gxb, lam, ts):
    t = proj.shape[0]
    w = LRU_WIDTH
    nt = seq // ts
    vec = lambda v: v.reshape(1, w).astype(F32)
    full = lambda shape: pl.BlockSpec(shape, lambda b, s: (0,) * len(shape))
    return pl.pallas_call(
        functools.partial(_lru_body, ts=ts),
        grid=(bsz, nt),
        in_specs=[pl.BlockSpec((ts, w), lambda b, s: (b * nt + s, 0)),
                  pl.BlockSpec((ts, w), lambda b, s: (b * nt + s, 1)),
                  full((4, w)), full((1, w)),
                  full((LRU_HEADS, LRU_HEAD_DIM, LRU_HEAD_DIM)), full((1, w)),
                  full((LRU_HEADS, LRU_HEAD_DIM, LRU_HEAD_DIM)), full((1, w)),
                  full((1, w))],
        out_specs=pl.BlockSpec((ts, w), lambda b, s: (b * nt + s, 0)),
        out_shape=jax.ShapeDtypeStruct((t, w), BF16),
        scratch_shapes=[pltpu.VMEM((SUBLANE + ts, w), F32),
                        pltpu.VMEM((ts, w), F32),
                        pltpu.VMEM((ts, w), F32),
                        pltpu.VMEM((SUBLANE, w), F32)],
        compiler_params=_cparams(("parallel", "arbitrary")),
        name="rg_lru",
    )(proj, proj, cw.astype(F32), vec(cb), gaw.astype(BF16), vec(gab), gxw.astype(BF16), vec(gxb), vec(lam))


def _ssd_body(z_ref, xbc_ref, dt_ref, cw_ref, cb_ref, dtb_ref, alog_ref, dexp_ref, ng_ref, eexp_ref,
              o_ref, pad_ref, st_ref):
    L = SSD_CHUNK

    @pl.when(pl.program_id(1) == 0)
    def _():
        pad_ref[0:SUBLANE, :] = jnp.zeros((SUBLANE, SSD_CONV_DIM), F32)
        st_ref[...] = jnp.zeros((SSD_STATE, SSD_INNER), F32)

    pad_ref[SUBLANE:SUBLANE + L, :] = xbc_ref[...]
    conv = cb_ref[...] + cw_ref[3:4, :] * xbc_ref[...]
    for k in range(3):
        conv = conv + cw_ref[k:k + 1, :] * pad_ref[5 + k:5 + k + L, :]
    pad_ref[0:SUBLANE, :] = xbc_ref[L - SUBLANE:L, :]
    xbc = _silu(conv)
    xs = xbc[:, :SSD_INNER]

    dt = _softplus(dt_ref[...] + dtb_ref[...])
    da = dt * (-jnp.exp(alog_ref[...]))
    rr = lax.broadcasted_iota(jnp.int32, (L, L), 0)
    cc = lax.broadcasted_iota(jnp.int32, (L, L), 1)
    causal = rr >= cc
    tril = jnp.where(causal, 1.0, 0.0).astype(BF16)
    cum = _dot_exact_lhs01(tril, da, 3)
    cum_t = cum.T
    cum_last = cum[L - 1:L, :]
    w_end = jnp.exp(cum_last - cum) * dt
    ecum = jnp.exp(cum)
    stacked = jnp.concatenate([dt, w_end, ecum], axis=0)
    expd = _dot_exact_rhs01(stacked, eexp_ref[...], 2)
    dt_e = expd[0:L]
    wend_e = expd[L:2 * L]
    ecum_e = expd[2 * L:3 * L]
    xdt = xs * dt_e
    xw = (xs * wend_e).astype(BF16)

    lane = lax.broadcasted_iota(jnp.int32, (L, LANE), 1)
    left = lane < SSD_HEAD_DIM
    y_parts = []
    for g in range(SSD_GROUPS):
        bg = xbc[:, SSD_INNER + g * SSD_STATE:SSD_INNER + (g + 1) * SSD_STATE]
        cg = xbc[:, SSD_INNER + SSD_GROUPS * SSD_STATE + g * SSD_STATE:
                 SSD_INNER + SSD_GROUPS * SSD_STATE + (g + 1) * SSD_STATE]
        bg16 = bg.astype(BF16)
        cg16 = cg.astype(BF16)
        cb = lax.dot_general(cg16, bg16, (((1,), (1,)), ((), ())), preferred_element_type=F32)
        for pair in range(SSD_HPG // 2):
            tile = g * (SSD_HPG // 2) + pair
            xp = xdt[:, tile * LANE:(tile + 1) * LANE]
            acc = None
            for side in range(2):
                j = 2 * tile + side
                seg = cum[:, j:j + 1] - cum_t[j:j + 1, :]
                decay = jnp.where(causal, jnp.exp(jnp.minimum(seg, 0.0)), 0.0)
                m = (cb * decay).astype(BF16)
                xsel = jnp.where(left if side == 0 else jnp.logical_not(left), xp, 0.0).astype(BF16)
                d = jnp.dot(m, xsel, preferred_element_type=F32)
                acc = d if acc is None else acc + d
            y_parts.append(acc)
    y_diag = jnp.concatenate(y_parts, axis=1)

    half = SSD_HPG * SSD_HEAD_DIM
    y_off_parts = []
    for g in range(SSD_GROUPS):
        bg = xbc[:, SSD_INNER + g * SSD_STATE:SSD_INNER + (g + 1) * SSD_STATE]
        cg = xbc[:, SSD_INNER + SSD_GROUPS * SSD_STATE + g * SSD_STATE:
                 SSD_INNER + SSD_GROUPS * SSD_STATE + (g + 1) * SSD_STATE]
        gs = slice(g * half, (g + 1) * half)
        prev = st_ref[:, gs]
        y_off_parts.append(jnp.dot(cg.astype(BF16), prev.astype(BF16), preferred_element_type=F32))
        new_states = jnp.dot(bg.T.astype(BF16), xw[:, gs], preferred_element_type=F32)
        st_ref[:, gs] = prev * ecum_e[L - 1:L, gs] + new_states
    y_off = jnp.concatenate(y_off_parts, axis=1) * ecum_e

    y = y_diag + y_off + dexp_ref[...] * xs
    yz = y * _silu(z_ref[...])
    ms = jnp.mean(yz * yz, axis=-1, keepdims=True)
    o_ref[...] = (yz * lax.rsqrt(ms + NORM_EPS) * ng_ref[...]).astype(o_ref.dtype)


def _ssd_branch(proj, bsz, seq, cw, cb, dt_bias, a_log, d_skip, norm_g):
    t = proj.shape[0]
    L = SSD_CHUNK
    nc = seq // L
    pad16 = lambda v: jnp.zeros((1, LANE), F32).at[0, :SSD_HEADS].set(v.astype(F32))
    d_exp = jnp.repeat(d_skip.astype(F32), SSD_HEAD_DIM).reshape(1, SSD_INNER)
    e_exp = (jnp.arange(LANE)[:, None] == (jnp.arange(SSD_INNER)[None, :] // SSD_HEAD_DIM)).astype(BF16)
    full = lambda shape: pl.BlockSpec(shape, lambda b, c: (0,) * len(shape))
    return pl.pallas_call(
        _ssd_body,
        grid=(bsz, nc),
        in_specs=[pl.BlockSpec((L, SSD_INNER), lambda b, c: (b * nc + c, 2)),
                  pl.BlockSpec((L, SSD_CONV_DIM), lambda b, c: (b * nc + c, 2)),
                  pl.BlockSpec((L, LANE), lambda b, c: (b * nc + c, EVEN_MAIN // LANE)),
                  full((4, SSD_CONV_DIM)), full((1, SSD_CONV_DIM)),
                  full((1, LANE)), full((1, LANE)), full((1, SSD_INNER)), full((1, SSD_INNER)),
                  full((LANE, SSD_INNER))],
        out_specs=pl.BlockSpec((L, SSD_INNER), lambda b, c: (b * nc + c, 0)),
        out_shape=jax.ShapeDtypeStruct((t, SSD_INNER), BF16),
        scratch_shapes=[pltpu.VMEM((SUBLANE + L, SSD_CONV_DIM), F32),
                        pltpu.VMEM((SSD_STATE, SSD_INNER), F32)],
        compiler_params=_cparams(("parallel", "arbitrary")),
        name="ssd",
    )(proj, proj, proj, cw.astype(F32), cb.reshape(1, -1).astype(F32), pad16(dt_bias), pad16(a_log),
      d_exp, norm_g.reshape(1, -1).astype(F32), e_exp)


def _even_out_body(ya_ref, yb_ref, wa_ref, wb_ref, res_ref, o_ref):
    o_ref[...] = (res_ref[...]
                  + jnp.dot(ya_ref[...], wa_ref[...], preferred_element_type=F32)
                  + jnp.dot(yb_ref[...], wb_ref[...], preferred_element_type=F32))


def _even_out(ya, yb, w_out, res, tm):
    t, d = res.shape
    w16 = w_out.astype(BF16)
    row = lambda n: pl.BlockSpec((tm, n), lambda i: (i, 0))
    return pl.pallas_call(
        _even_out_body,
        grid=(t // tm,),
        in_specs=[row(LRU_WIDTH), row(SSD_INNER),
                  pl.BlockSpec((LRU_WIDTH, d), lambda i: (0, 0)),
                  pl.BlockSpec((SSD_INNER, d), lambda i: (0, 0)),
                  row(d)],
        out_specs=row(d),
        out_shape=jax.ShapeDtypeStruct((t, d), F32),
        compiler_params=_cparams(("parallel",)),
        name="even_out",
    )(ya, yb, w16[:LRU_WIDTH], w16[LRU_WIDTH:], res)


def _odd_body(bg_ref, cg_ref, v_ref, cw_ref, w_ref, res_ref, o_ref, pad_ref, *, ts):
    w = D_MODEL

    @pl.when(pl.program_id(1) == 0)
    def _():
        pad_ref[0:SUBLANE, :] = jnp.zeros((SUBLANE, w), F32)

    cv = cg_ref[...] * v_ref[...]
    pad_ref[SUBLANE:SUBLANE + ts, :] = cv
    conv = cw_ref[2:3, :] * cv
    for k in range(2):
        conv = conv + cw_ref[k:k + 1, :] * pad_ref[6 + k:6 + k + ts, :]
    pad_ref[0:SUBLANE, :] = cv[ts - SUBLANE:ts, :]
    u = (bg_ref[...] * conv).astype(BF16)
    o_ref[...] = res_ref[...] + jnp.dot(u, w_ref[...], preferred_element_type=F32)


def _odd_mix_out(proj, bsz, seq, cw, w_out, res, ts):
    t, d = res.shape
    nt = seq // ts
    col = lambda c: pl.BlockSpec((ts, d), lambda b, s: (b * nt + s, c))
    return pl.pallas_call(
        functools.partial(_odd_body, ts=ts),
        grid=(bsz, nt),
        in_specs=[col(0), col(1), col(2),
                  pl.BlockSpec((3, d), lambda b, s: (0, 0)),
                  pl.BlockSpec((d, d), lambda b, s: (0, 0)),
                  col(0)],
        out_specs=col(0),
        out_shape=jax.ShapeDtypeStruct((t, d), F32),
        scratch_shapes=[pltpu.VMEM((SUBLANE + ts, d), F32)],
        compiler_params=_cparams(("parallel", "arbitrary")),
        name="odd_mix_out",
    )(proj, proj, proj, cw.astype(F32), w_out.astype(BF16), res)


def _topk_body(q_ref, sk_ref, r2_ref, e2_ref, n1_ref, c1_ref, rk1_ref, sv_ref, nz_ref, *, tt):
    nl = tt // LANE
    iota_n = lax.broadcasted_iota(jnp.int32, (PEER_NKEYS, LANE), 0).astype(F32)

    def top16(x0, p, h, lsl):
        def rnd(r, carry):
            x, rank, rf = carry
            m = jnp.max(x, axis=0, keepdims=True)
            idx = jnp.min(jnp.where(x == m, iota_n, float(PEER_NKEYS)), axis=0, keepdims=True)
            hit = iota_n == idx
            sv_ref[p, r, h:h + 1, lsl] = m
            return jnp.where(hit, -jnp.inf, x), jnp.where(hit, rf, rank), rf + 1.0

        init = (x0, jnp.full((PEER_NKEYS, LANE), NOT_TOP, F32), jnp.zeros((1, LANE), F32))
        _, rank, _ = lax.fori_loop(0, PEER_TOPK, rnd, init)
        return rank

    for h in range(PEER_HEADS):
        for p in range(2):
            qp = q_ref[:, (2 * h + p) * PEER_HALF:(2 * h + p + 1) * PEER_HALF]
            s = lax.dot_general(sk_ref[h, p], qp, (((1,), (1,)), ((), ())),
                                preferred_element_type=F32)
            for lt in range(nl):
                lsl = slice(lt * LANE, (lt + 1) * LANE)
                x0 = s[:, lsl]
                rank = top16(x0, p, h, lsl)
                if p == 0:
                    rk1_ref[h, :, lsl] = rank
                else:
                    r2_ref[h, :, lsl] = rank.astype(BF16)
                    e2_ref[h, :, lsl] = jnp.exp(x0 - sv_ref[1, 0, h:h + 1, lsl]).astype(BF16)

    for lt in range(nl):
        lsl = slice(lt * LANE, (lt + 1) * LANE)
        sv1 = [sv_ref[0, i, :, lsl] for i in range(PEER_TOPK)]
        sv2 = [sv_ref[1, j, :, lsl] for j in range(PEER_TOPK)]
        cand = {c: sv1[c[0]] + sv2[c[1]] for c in _CELLS}
        beat = {c: None for c in _CELLS}
        lost = {c: None for c in _CELLS}
        nfirst = {c: 0 for c in _CELLS}
        add = lambda a, b: b if a is None else a + b
        for ci, c in enumerate(_CELLS):
            for c2 in _CELLS[ci + 1:]:
                if c2[0] > c[0] and c2[1] < c[1]:
                    m = jnp.where(cand[c] >= cand[c2], 1.0, 0.0)
                    beat[c2] = add(beat[c2], m)
                    lost[c] = add(lost[c], m)
                    nfirst[c] += 1
        e1 = [jnp.exp(sv1[i] - sv1[0]) for i in range(PEER_TOPK)]
        e2 = [jnp.exp(sv2[j] - sv2[0]) for j in range(PEER_TOPK)]
        nsel = [None] * PEER_TOPK
        zsum = None
        for c in _CELLS:
            base = float((c[0] + 1) * (c[1] + 1) - 1 + nfirst[c])
            cnt = base
            if beat[c] is not None:
                cnt = cnt + beat[c]
            if lost[c] is not None:
                cnt = cnt - lost[c]
            if isinstance(cnt, float):
                sel = jnp.full((PEER_HEADS, LANE), 1.0 if cnt < PEER_TOPK else 0.0, F32)
            else:
                sel = jnp.where(cnt < float(PEER_TOPK), 1.0, 0.0)
            nsel[c[0]] = add(nsel[c[0]], sel)
            zsum = add(zsum, sel * (e1[c[0]] * e2[c[1]]))
        for i in range(PEER_TOPK):
            nz_ref[0, i, :, lsl] = nsel[i]
            nz_ref[1, i, :, lsl] = e1[i] / zsum

    for h in range(PEER_HEADS):
        for lt in range(nl):
            lsl = slice(lt * LANE, (lt + 1) * LANE)
            rank1 = rk1_ref[h, :, lsl]
            nrow = jnp.zeros((PEER_NKEYS, LANE), F32)
            crow = jnp.zeros((PEER_NKEYS, LANE), F32)
            for i in range(PEER_TOPK):
                hit = rank1 == float(i)
                nrow = jnp.where(hit, nz_ref[0, i, h:h + 1, lsl], nrow)
                crow = jnp.where(hit, nz_ref[1, i, h:h + 1, lsl], crow)
            n1_ref[h, :, lsl] = nrow
            c1_ref[h, :, lsl] = crow


def _peer_topk(q, sub_keys, tt):
    t = q.shape[0]
    hk = (PEER_HEADS, PEER_NKEYS, t)
    blk = pl.BlockSpec((PEER_HEADS, PEER_NKEYS, tt), lambda i: (0, 0, i))
    return pl.pallas_call(
        functools.partial(_topk_body, tt=tt),
        grid=(t // tt,),
        in_specs=[pl.BlockSpec((tt, 2 * PEER_HEADS * PEER_HALF), lambda i: (i, 0)),
                  pl.BlockSpec((PEER_HEADS, 2, PEER_NKEYS, PEER_HALF), lambda i: (0, 0, 0, 0))],
        out_specs=[blk, blk, blk, blk],
        out_shape=[jax.ShapeDtypeStruct(hk, BF16), jax.ShapeDtypeStruct(hk, BF16),
                   jax.ShapeDtypeStruct(hk, F32), jax.ShapeDtypeStruct(hk, F32)],
        scratch_shapes=[pltpu.VMEM((PEER_HEADS, PEER_NKEYS, tt), F32),
                        pltpu.VMEM((2, PEER_TOPK, PEER_HEADS, tt), F32),
                        pltpu.VMEM((2, PEER_TOPK, PEER_HEADS, tt), F32)],
        compiler_params=_cparams(("parallel",)),
        name="peer_topk",
    )(q, sub_keys.astype(BF16))


PACK = 16


def _peer_mix_body(xn_ref, u_ref, vt_ref, r2_ref, e2_ref, n1_ref, c1_ref, res_ref, o_ref,
                   acc_ref, w_ref, *, tt, eb):
    k = pl.program_id(1)
    nl = tt // LANE
    na = eb // PEER_NKEYS
    grp = PEER_NKEYS // PACK

    @pl.when(k == 0)
    def _():
        acc_ref[...] = jnp.zeros_like(acc_ref)

    act_t = lax.dot_general(u_ref[...], xn_ref[...], (((1,), (1,)), ((), ())),
                            preferred_element_type=F32)
    for al in range(na):
        for lt in range(nl):
            lsl = slice(lt * LANE, (lt + 1) * LANE)
            gate = jnp.zeros((grp, PACK, LANE), BF16)
            for h in range(PEER_HEADS):
                nb = jnp.broadcast_to(n1_ref[h, al:al + 1, lsl], (PACK, LANE)).astype(BF16)
                cb = jnp.broadcast_to(c1_ref[h, al:al + 1, lsl], (PACK, LANE)).astype(BF16)
                r2 = r2_ref[h, :, :, lsl]
                e2 = e2_ref[h, :, :, lsl]
                gate = gate + jnp.where(r2 < nb[None], e2, jnp.zeros_like(e2)) * cb[None]
            act = _gelu_tanh(act_t[al * PEER_NKEYS:(al + 1) * PEER_NKEYS, lsl]).astype(BF16)
            w_ref[al * PEER_NKEYS:(al + 1) * PEER_NKEYS, lsl] = (
                act.reshape(grp, PACK, LANE) * gate).reshape(PEER_NKEYS, LANE)
    acc_ref[...] += jnp.dot(vt_ref[...], w_ref[...], preferred_element_type=F32)

    @pl.when(k == pl.num_programs(1) - 1)
    def _():
        o_ref[...] = res_ref[...] + acc_ref[...].T


def _peer_mix(xn, u16, vt16, r2, e2, n1, c1, res, tt, eb):
    t, d = res.shape
    ne = u16.shape[0]
    grp = PEER_NKEYS // PACK
    na = eb // PEER_NKEYS
    r2v = r2.reshape(PEER_HEADS, grp, PACK, t)
    e2v = e2.reshape(PEER_HEADS, grp, PACK, t)
    rank_blk = pl.BlockSpec((PEER_HEADS, grp, PACK, tt), lambda i, k: (0, 0, 0, i))
    row_blk = pl.BlockSpec((PEER_HEADS, na, tt), lambda i, k: (0, k, i))
    return pl.pallas_call(
        functools.partial(_peer_mix_body, tt=tt, eb=eb),
        grid=(t // tt, ne // eb),
        in_specs=[pl.BlockSpec((tt, d), lambda i, k: (i, 0)),
                  pl.BlockSpec((eb, d), lambda i, k: (k, 0)),
                  pl.BlockSpec((d, eb), lambda i, k: (0, k)),
                  rank_blk, rank_blk, row_blk, row_blk,
                  pl.BlockSpec((tt, d), lambda i, k: (i, 0))],
        out_specs=pl.BlockSpec((tt, d), lambda i, k: (i, 0)),
        out_shape=jax.ShapeDtypeStruct((t, d), F32),
        scratch_shapes=[pltpu.VMEM((d, tt), F32), pltpu.VMEM((eb, tt), BF16)],
        compiler_params=_cparams(("parallel", "arbitrary")),
        name="peer_mix",
    )(xn, u16, vt16, r2v, e2v, n1, c1, res)


def _norm_body(x_ref, g_ref, o_ref):
    x = x_ref[...]
    ms = jnp.mean(x * x, axis=-1, keepdims=True)
    o_ref[...] = x * lax.rsqrt(ms + NORM_EPS) * g_ref[...]


def _rmsnorm(x, g, tm):
    t, d = x.shape
    return pl.pallas_call(
        _norm_body,
        grid=(t // tm,),
        in_specs=[pl.BlockSpec((tm, d), lambda i: (i, 0)), pl.BlockSpec((1, d), lambda i: (0, 0))],
        out_specs=pl.BlockSpec((tm, d), lambda i: (i, 0)),
        out_shape=jax.ShapeDtypeStruct((t, d), F32),
        compiler_params=_cparams(("parallel",)),
        name="final_norm",
    )(x, g.reshape(1, d).astype(F32))


def _tiles(bsz, seq):
    t = bsz * seq
    pick = lambda n, opts: next(o for o in opts if n % o == 0)
    return dict(tm=pick(t, (256, 128)), ts=pick(seq, (512, 256, 128)),
                tk=pick(t, (256, 128)), tt=pick(t, (512, 256, 128)), eb=1024)


def kernel(x, even_norm_g, even_w_in, lru_conv_w, lru_conv_b, lru_gate_a_w, lru_gate_a_b, lru_gate_x_w,
           lru_gate_x_b, lru_lambda, ssd_conv_w, ssd_conv_b, ssd_dt_bias, ssd_a_log, ssd_d, ssd_norm_g,
           even_w_out, odd_norm_g, odd_w_in, odd_conv_w, odd_w_out, ffn_norm_g, peer_w_query,
           peer_sub_keys, peer_u, peer_v, final_norm_g):
    bsz, seq, d = x.shape
    assert d == D_MODEL and seq % SSD_CHUNK == 0
    depth = ffn_norm_g.shape[0]
    cfg = _tiles(bsz, seq)
    h = x.reshape(bsz * seq, d)
    for layer in range(depth):
        i = layer // 2
        if layer % 2 == 0:
            w_in = jnp.pad(even_w_in[i].astype(BF16), ((0, 0), (0, EVEN_PAD - even_w_in.shape[2])))
            proj = _norm_matmul(h, even_norm_g[i], w_in, F32, cfg["tm"])
            ya = _lru_branch(proj, bsz, seq, lru_conv_w[i], lru_conv_b[i], lru_gate_a_w[i], lru_gate_a_b[i],
                             lru_gate_x_w[i], lru_gate_x_b[i], lru_lambda[i], cfg["ts"])
            yb = _ssd_branch(proj, bsz, seq, ssd_conv_w[i], ssd_conv_b[i], ssd_dt_bias[i], ssd_a_log[i],
                             ssd_d[i], ssd_norm_g[i])
            h = _even_out(ya, yb, even_w_out[i], h, cfg["tm"])
        else:
            proj = _norm_matmul(h, odd_norm_g[i], odd_w_in[i].astype(BF16), F32, cfg["tm"])
            h = _odd_mix_out(proj, bsz, seq, odd_conv_w[i], odd_w_out[i], h, cfg["ts"])
        q, xn = _norm_matmul(h, ffn_norm_g[layer], peer_w_query[layer].astype(BF16), BF16, cfg["tm"],
                             want_xn=True)
        r2, e2, n1, c1 = _peer_topk(q, peer_sub_keys[layer], cfg["tk"])
        h = _peer_mix(xn, peer_u[layer].astype(BF16), peer_v[layer].astype(BF16).T, r2, e2, n1, c1, h,
                      cfg["tt"], cfg["eb"])
    return _rmsnorm(h, final_norm_g, cfg["tm"]).reshape(bsz, seq, d)
```

```python
import functools

import jax
import jax.numpy as jnp
from jax import lax
from jax.experimental import pallas as pl
from jax.experimental.pallas import tpu as pltpu

F32 = jnp.float32
BF16 = jnp.bfloat16

D_MODEL = 1024
NORM_EPS = 1e-6

LRU_WIDTH = 1024
LRU_HEADS = 8
LRU_HEAD_DIM = 128
LRU_C = 8.0

SSD_INNER = 1024
SSD_HEAD_DIM = 64
SSD_HEADS = 16
SSD_GROUPS = 2
SSD_HPG = 8
SSD_STATE = 128
SSD_CHUNK = 128
SSD_CONV_DIM = SSD_INNER + 2 * SSD_GROUPS * SSD_STATE

EVEN_MAIN = 2 * LRU_WIDTH + SSD_INNER + SSD_CONV_DIM
EVEN_PAD = EVEN_MAIN + 128

PEER_HEADS = 8
PEER_NKEYS = 128
PEER_NEXPERTS = PEER_NKEYS * PEER_NKEYS
PEER_TOPK = 16
PEER_HALF = 128

LANE = 128
SUBLANE = 8
VMEM_LIMIT = 56 * 1024 * 1024

NOT_TOP = 100.0

_CELLS = [(i, j) for i in range(PEER_TOPK) for j in range(PEER_TOPK) if (i + 1) * (j + 1) <= PEER_TOPK]


def _cparams(sem):
    return pltpu.CompilerParams(dimension_semantics=sem, vmem_limit_bytes=VMEM_LIMIT)


def _split_bf16(x, terms):
    parts = []
    rem = x
    for _ in range(terms):
        p = rem.astype(BF16)
        parts.append(p)
        rem = rem - p.astype(F32)
    return parts


def _dot_exact_rhs01(x, m01, terms):
    acc = None
    for p in _split_bf16(x, terms):
        d = jnp.dot(p, m01, preferred_element_type=F32)
        acc = d if acc is None else acc + d
    return acc


def _dot_exact_lhs01(m01, x, terms):
    acc = None
    for p in _split_bf16(x, terms):
        d = jnp.dot(m01, p, preferred_element_type=F32)
        acc = d if acc is None else acc + d
    return acc


def _softplus(x):
    return jnp.maximum(x, 0.0) + jnp.log1p(jnp.exp(-jnp.abs(x)))


def _expm1(x):
    u = jnp.exp(x)
    um1 = u - 1.0
    return jnp.where(um1 == 0.0, x, um1 * x / jnp.log(jnp.where(um1 == 0.0, 2.0, u)))


def _sigmoid(x):
    return 1.0 / (1.0 + jnp.exp(-x))


def _silu(x):
    return x * _sigmoid(x)


def _gelu_tanh(x):
    c = 0.7978845608028654
    return 0.5 * x * (1.0 + jnp.tanh(c * (x + 0.044715 * (x * x * x))))


def _nmm_body(x_ref, g_ref, w_ref, o_ref, *xn_out):
    x = x_ref[...]
    ms = jnp.mean(x * x, axis=-1, keepdims=True)
    xn = (x * lax.rsqrt(ms + NORM_EPS) * g_ref[...]).astype(BF16)
    o_ref[...] = jnp.dot(xn, w_ref[...], preferred_element_type=F32).astype(o_ref.dtype)
    if xn_out:
        xn_out[0][...] = xn


def _norm_matmul(x, g, w_bf16, out_dtype, tm, want_xn=False):
    t, d = x.shape
    n = w_bf16.shape[1]
    out_shape = [jax.ShapeDtypeStruct((t, n), out_dtype)]
    out_specs = [pl.BlockSpec((tm, n), lambda i: (i, 0))]
    if want_xn:
        out_shape.append(jax.ShapeDtypeStruct((t, d), BF16))
        out_specs.append(pl.BlockSpec((tm, d), lambda i: (i, 0)))
    res = pl.pallas_call(
        _nmm_body,
        grid=(t // tm,),
        in_specs=[pl.BlockSpec((tm, d), lambda i: (i, 0)),
                  pl.BlockSpec((1, d), lambda i: (0, 0)),
                  pl.BlockSpec((d, n), lambda i: (0, 0))],
        out_specs=out_specs,
        out_shape=out_shape,
        compiler_params=_cparams(("parallel",)),
        name="norm_matmul",
    )(x, g.reshape(1, d), w_bf16)
    return res if want_xn else res[0]


def _lru_body(gate_ref, x_ref, cw_ref, cb_ref, gaw_ref, gab_ref, gxw_ref, gxb_ref, lam_ref,
              o_ref, xpad_ref, a_ref, b_ref, carry_ref, *, ts):
    w = LRU_WIDTH

    @pl.when(pl.program_id(1) == 0)
    def _():
        xpad_ref[0:SUBLANE, :] = jnp.zeros((SUBLANE, w), F32)
        carry_ref[...] = jnp.zeros((SUBLANE, w), F32)

    xpad_ref[SUBLANE:SUBLANE + ts, :] = x_ref[...]
    xa = cb_ref[...] + cw_ref[3:4, :] * x_ref[...]
    for k in range(3):
        xa = xa + cw_ref[k:k + 1, :] * xpad_ref[5 + k:5 + k + ts, :]
    xpad_ref[0:SUBLANE, :] = x_ref[ts - SUBLANE:ts, :]

    sp = _softplus(-lam_ref[...])
    for hd in range(LRU_HEADS):
        sl = slice(hd * LRU_HEAD_DIM, (hd + 1) * LRU_HEAD_DIM)
        xh = xa[:, sl]
        xh16 = xh.astype(BF16)
        r = _sigmoid(jnp.dot(xh16, gaw_ref[hd], preferred_element_type=F32) + gab_ref[:, sl])
        ig = _sigmoid(jnp.dot(xh16, gxw_ref[hd], preferred_element_type=F32) + gxb_ref[:, sl])
        log_a = (-LRU_C) * r * sp[:, sl]
        a_ref[:, sl] = jnp.exp(log_a)
        mult = jnp.sqrt(jnp.maximum(-_expm1(2.0 * log_a), 0.0))
        b_ref[:, sl] = mult * (ig * xh)

    row = lax.broadcasted_iota(jnp.int32, (SUBLANE, w), 0)

    def step(i, carry):
        r0 = pl.multiple_of(i * SUBLANE, SUBLANE)
        a = a_ref[pl.ds(r0, SUBLANE), :]
        b = b_ref[pl.ds(r0, SUBLANE), :]
        for d in (1, 2, 4):
            keep = row >= d
            a_sh = jnp.where(keep, pltpu.roll(a, d, 0), 1.0)
            b_sh = jnp.where(keep, pltpu.roll(b, d, 0), 0.0)
            b = a * b_sh + b
            a = a * a_sh
        hcur = b + a * carry
        b_ref[pl.ds(r0, SUBLANE), :] = hcur
        return jnp.broadcast_to(hcur[SUBLANE - 1:SUBLANE, :], (SUBLANE, w))

    carry = lax.fori_loop(0, ts // SUBLANE, step, carry_ref[...], unroll=2)
    carry_ref[...] = carry
    o_ref[...] = (_gelu_tanh(gate_ref[...]) * b_ref[...]).astype(o_ref.dtype)


def _lru_branch(proj, bsz, seq, cw, cb, gaw, gab, gxw, gxb, lam, ts):
    t = proj.shape[0]
    w = LRU_WIDTH
    nt = seq // ts
    vec = lambda v: v.reshape(1, w).astype(F32)
    full = lambda shape: pl.BlockSpec(shape, lambda b, s: (0,) * len(shape))
    return pl.pallas_call(
        functools.partial(_lru_body, ts=ts),
        grid=(bsz, nt),
        in_specs=[pl.BlockSpec((ts, w), lambda b, s: (b * nt + s, 0)),
                  pl.BlockSpec((ts, w), lambda b, s: (b * nt + s, 1)),
                  full((4, w)), full((1, w)),
                  full((LRU_HEADS, LRU_HEAD_DIM, LRU_HEAD_DIM)), full((1, w)),
                  full((LRU_HEADS, LRU_HEAD_DIM, LRU_HEAD_DIM)), full((1, w)),
                  full((1, w))],
        out_specs=pl.BlockSpec((ts, w), lambda b, s: (b * nt + s, 0)),
        out_shape=jax.ShapeDtypeStruct((t, w), BF16),
        scratch_shapes=[pltpu.VMEM((SUBLANE + ts, w), F32),
                        pltpu.VMEM((ts, w), F32),
                        pltpu.VMEM((ts, w), F32),
                        pltpu.VMEM((SUBLANE, w), F32)],
        compiler_params=_cparams(("parallel", "arbitrary")),
        name="rg_lru",
    )(proj, proj, cw.astype(F32), vec(cb), gaw.astype(BF16), vec(gab), gxw.astype(BF16), vec(gxb), vec(lam))


def _ssd_body(z_ref, xbc_ref, dt_ref, cw_ref, cb_ref, dtb_ref, alog_ref, dexp_ref, ng_ref, eexp_ref,
              o_ref, pad_ref, st_ref):
    L = SSD_CHUNK

    @pl.when(pl.program_id(1) == 0)
    def _():
        pad_ref[0:SUBLANE, :] = jnp.zeros((SUBLANE, SSD_CONV_DIM), F32)
        st_ref[...] = jnp.zeros((SSD_STATE, SSD_INNER), F32)

    pad_ref[SUBLANE:SUBLANE + L, :] = xbc_ref[...]
    conv = cb_ref[...] + cw_ref[3:4, :] * xbc_ref[...]
    for k in range(3):
        conv = conv + cw_ref[k:k + 1, :] * pad_ref[5 + k:5 + k + L, :]
    pad_ref[0:SUBLANE, :] = xbc_ref[L - SUBLANE:L, :]
    xbc = _silu(conv)
    xs = xbc[:, :SSD_INNER]

    dt = _softplus(dt_ref[...] + dtb_ref[...])
    da = dt * (-jnp.exp(alog_ref[...]))
    rr = lax.broadcasted_iota(jnp.int32, (L, L), 0)
    cc = lax.broadcasted_iota(jnp.int32, (L, L), 1)
    causal = rr >= cc
    tril = jnp.where(causal, 1.0, 0.0).astype(BF16)
    cum = _dot_exact_lhs01(tril, da, 3)
    cum_t = cum.T
    cum_last = cum[L - 1:L, :]
    w_end = jnp.exp(cum_last - cum) * dt
    ecum = jnp.exp(cum)
    stacked = jnp.concatenate([dt, w_end, ecum], axis=0)
    expd = _dot_exact_rhs01(stacked, eexp_ref[...], 2)
    dt_e = expd[0:L]
    wend_e = expd[L:2 * L]
    ecum_e = expd[2 * L:3 * L]
    xdt = xs * dt_e
    xw = (xs * wend_e).astype(BF16)

    lane = lax.broadcasted_iota(jnp.int32, (L, LANE), 1)
    left = lane < SSD_HEAD_DIM
    y_parts = []
    for g in range(SSD_GROUPS):
        bg = xbc[:, SSD_INNER + g * SSD_STATE:SSD_INNER + (g + 1) * SSD_STATE]
        cg = xbc[:, SSD_INNER + SSD_GROUPS * SSD_STATE + g * SSD_STATE:
                 SSD_INNER + SSD_GROUPS * SSD_STATE + (g + 1) * SSD_STATE]
        bg16 = bg.astype(BF16)
        cg16 = cg.astype(BF16)
        cb = lax.dot_general(cg16, bg16, (((1,), (1,)), ((), ())), preferred_element_type=F32)
        for pair in range(SSD_HPG // 2):
            tile = g * (SSD_HPG // 2) + pair
            xp = xdt[:, tile * LANE:(tile + 1) * LANE]
            acc = None
            for side in range(2):
                j = 2 * tile + side
                seg = cum[:, j:j + 1] - cum_t[j:j + 1, :]
                decay = jnp.where(causal, jnp.exp(jnp.minimum(seg, 0.0)), 0.0)
                m = (cb * decay).astype(BF16)
                xsel = jnp.where(left if side == 0 else jnp.logical_not(left), xp, 0.0).astype(BF16)
                d = jnp.dot(m, xsel, preferred_element_type=F32)
                acc = d if acc is None else acc + d
            y_parts.append(acc)
    y_diag = jnp.concatenate(y_parts, axis=1)

    half = SSD_HPG * SSD_HEAD_DIM
    y_off_parts = []
    for g in range(SSD_GROUPS):
        bg = xbc[:, SSD_INNER + g * SSD_STATE:SSD_INNER + (g + 1) * SSD_STATE]
        cg = xbc[:, SSD_INNER + SSD_GROUPS * SSD_STATE + g * SSD_STATE:
                 SSD_INNER + SSD_GROUPS * SSD_STATE + (g + 1) * SSD_STATE]
        gs = slice(g * half, (g + 1) * half)
        prev = st_ref[:, gs]
        y_off_parts.append(jnp.dot(cg.astype(BF16), prev.astype(BF16), preferred_element_type=F32))
        new_states = jnp.dot(bg.T.astype(BF16), xw[:, gs], preferred_element_type=F32)
        st_ref[:, gs] = prev * ecum_e[L - 1:L, gs] + new_states
    y_off = jnp.concatenate(y_off_parts, axis=1) * ecum_e

    y = y_diag + y_off + dexp_ref[...] * xs
    yz = y * _silu(z_ref[...])
    ms = jnp.mean(yz * yz, axis=-1, keepdims=True)
    o_ref[...] = (yz * lax.rsqrt(ms + NORM_EPS) * ng_ref[...]).astype(o_ref.dtype)


def _ssd_branch(proj, bsz, seq, cw, cb, dt_bias, a_log, d_skip, norm_g):
    t = proj.shape[0]
    L = SSD_CHUNK
    nc = seq // L
    pad16 = lambda v: jnp.zeros((1, LANE), F32).at[0, :SSD_HEADS].set(v.astype(F32))
    d_exp = jnp.repeat(d_skip.astype(F32), SSD_HEAD_DIM).reshape(1, SSD_INNER)
    e_exp = (jnp.arange(LANE)[:, None] == (jnp.arange(SSD_INNER)[None, :] // SSD_HEAD_DIM)).astype(BF16)
    full = lambda shape: pl.BlockSpec(shape, lambda b, c: (0,) * len(shape))
    return pl.pallas_call(
        _ssd_body,
        grid=(bsz, nc),
        in_specs=[pl.BlockSpec((L, SSD_INNER), lambda b, c: (b * nc + c, 2)),
                  pl.BlockSpec((L, SSD_CONV_DIM), lambda b, c: (b * nc + c, 2)),
                  pl.BlockSpec((L, LANE), lambda b, c: (b * nc + c, EVEN_MAIN // LANE)),
                  full((4, SSD_CONV_DIM)), full((1, SSD_CONV_DIM)),
                  full((1, LANE)), full((1, LANE)), full((1, SSD_INNER)), full((1, SSD_INNER)),
                  full((LANE, SSD_INNER))],
        out_specs=pl.BlockSpec((L, SSD_INNER), lambda b, c: (b * nc + c, 0)),
        out_shape=jax.ShapeDtypeStruct((t, SSD_INNER), BF16),
        scratch_shapes=[pltpu.VMEM((SUBLANE + L, SSD_CONV_DIM), F32),
                        pltpu.VMEM((SSD_STATE, SSD_INNER), F32)],
        compiler_params=_cparams(("parallel", "arbitrary")),
        name="ssd",
    )(proj, proj, proj, cw.astype(F32), cb.reshape(1, -1).astype(F32), pad16(dt_bias), pad16(a_log),
      d_exp, norm_g.reshape(1, -1).astype(F32), e_exp)


def _even_out_body(ya_ref, yb_ref, wa_ref, wb_ref, res_ref, o_ref):
    o_ref[...] = (res_ref[...]
                  + jnp.dot(ya_ref[...], wa_ref[...], preferred_element_type=F32)
                  + jnp.dot(yb_ref[...], wb_ref[...], preferred_element_type=F32))


def _even_out(ya, yb, w_out, res, tm):
    t, d = res.shape
    w16 = w_out.astype(BF16)
    row = lambda n: pl.BlockSpec((tm, n), lambda i: (i, 0))
    return pl.pallas_call(
        _even_out_body,
        grid=(t // tm,),
        in_specs=[row(LRU_WIDTH), row(SSD_INNER),
                  pl.BlockSpec((LRU_WIDTH, d), lambda i: (0, 0)),
                  pl.BlockSpec((SSD_INNER, d), lambda i: (0, 0)),
                  row(d)],
        out_specs=row(d),
        out_shape=jax.ShapeDtypeStruct((t, d), F32),
        compiler_params=_cparams(("parallel",)),
        name="even_out",
    )(ya, yb, w16[:LRU_WIDTH], w16[LRU_WIDTH:], res)


def _odd_body(bg_ref, cg_ref, v_ref, cw_ref, w_ref, res_ref, o_ref, pad_ref, *, ts):
    w = D_MODEL

    @pl.when(pl.program_id(1) == 0)
    def _():
        pad_ref[0:SUBLANE, :] = jnp.zeros((SUBLANE, w), F32)

    cv = cg_ref[...] * v_ref[...]
    pad_ref[SUBLANE:SUBLANE + ts, :] = cv
    conv = cw_ref[2:3, :] * cv
    for k in range(2):
        conv = conv + cw_ref[k:k + 1, :] * pad_ref[6 + k:6 + k + ts, :]
    pad_ref[0:SUBLANE, :] = cv[ts - SUBLANE:ts, :]
    u = (bg_ref[...] * conv).astype(BF16)
    o_ref[...] = res_ref[...] + jnp.dot(u, w_ref[...], preferred_element_type=F32)


def _odd_mix_out(proj, bsz, seq, cw, w_out, res, ts):
    t, d = res.shape
    nt = seq // ts
    col = lambda c: pl.BlockSpec((ts, d), lambda b, s: (b * nt + s, c))
    return pl.pallas_call(
        functools.partial(_odd_body, ts=ts),
        grid=(bsz, nt),
        in_specs=[col(0), col(1), col(2),
                  pl.BlockSpec((3, d), lambda b, s: (0, 0)),
                  pl.BlockSpec((d, d), lambda b, s: (0, 0)),
                  col(0)],
        out_specs=col(0),
        out_shape=jax.ShapeDtypeStruct((t, d), F32),
        scratch_shapes=[pltpu.VMEM((SUBLANE + ts, d), F32)],
        compiler_params=_cparams(("parallel", "arbitrary")),
        name="odd_mix_out",
    )(proj, proj, proj, cw.astype(F32), w_out.astype(BF16), res)


def _topk_body(q_ref, sk_ref, r2_ref, e2_ref, n1_ref, c1_ref, rk_ref, sv_ref, nz_ref, s2_ref, *, tt):
    nl = tt // LANE
    iota_n = lax.broadcasted_iota(jnp.int32, (PEER_NKEYS, LANE), 0).astype(F32)

    def scores(h, p):
        qp = q_ref[:, (2 * h + p) * PEER_HALF:(2 * h + p + 1) * PEER_HALF]
        return lax.dot_general(sk_ref[h, p], qp, (((1,), (1,)), ((), ())),
                               preferred_element_type=F32)

    bad = jnp.zeros((1, tt), F32)
    for h in range(PEER_HEADS):
        for p in range(2):
            x = scores(h, p)
            if p == 1:
                s2_ref[h] = x
            rk_ref[p, h] = jnp.full((PEER_NKEYS, tt), NOT_TOP, F32)
            for r in range(PEER_TOPK):
                m = jnp.max(x, axis=0, keepdims=True)
                hit = x == m
                sv_ref[p, r, h:h + 1, :] = m
                pltpu.store(rk_ref.at[p, h], jnp.full((PEER_NKEYS, tt), float(r), F32), mask=hit)
                x = jnp.where(hit, -jnp.inf, x)
            cnt = jnp.sum(jnp.where(rk_ref[p, h] < float(PEER_TOPK), 1.0, 0.0), axis=0, keepdims=True)
            bad = bad + jnp.abs(cnt - float(PEER_TOPK))

    @pl.when(jnp.max(bad) > 0.0)
    def _():
        def top16(x0, p, h, lsl):
            def rnd(r, carry):
                x, rank, rf = carry
                m = jnp.max(x, axis=0, keepdims=True)
                idx = jnp.min(jnp.where(x == m, iota_n, float(PEER_NKEYS)), axis=0, keepdims=True)
                hit = iota_n == idx
                sv_ref[p, r, h:h + 1, lsl] = m
                return jnp.where(hit, -jnp.inf, x), jnp.where(hit, rf, rank), rf + 1.0

            init = (x0, jnp.full((PEER_NKEYS, LANE), NOT_TOP, F32), jnp.zeros((1, LANE), F32))
            return lax.fori_loop(0, PEER_TOPK, rnd, init)[1]

        for h in range(PEER_HEADS):
            for p in range(2):
                s = scores(h, p)
                for lt in range(nl):
                    lsl = slice(lt * LANE, (lt + 1) * LANE)
                    rk_ref[p, h, :, lsl] = top16(s[:, lsl], p, h, lsl)

    rk1_ref = rk_ref.at[0]
    for h in range(PEER_HEADS):
        r2_ref[h] = rk_ref[1, h].astype(BF16)
        e2_ref[h] = jnp.exp(s2_ref[h] - sv_ref[1, 0, h:h + 1, :]).astype(BF16)

    for lt in range(nl):
        lsl = slice(lt * LANE, (lt + 1) * LANE)
        sv1 = [sv_ref[0, i, :, lsl] for i in range(PEER_TOPK)]
        sv2 = [sv_ref[1, j, :, lsl] for j in range(PEER_TOPK)]
        cand = {c: sv1[c[0]] + sv2[c[1]] for c in _CELLS}
        beat = {c: None for c in _CELLS}
        lost = {c: None for c in _CELLS}
        nfirst = {c: 0 for c in _CELLS}
        add = lambda a, b: b if a is None else a + b
        for ci, c in enumerate(_CELLS):
            for c2 in _CELLS[ci + 1:]:
                if c2[0] > c[0] and c2[1] < c[1]:
                    m = jnp.where(cand[c] >= cand[c2], 1.0, 0.0)
                    beat[c2] = add(beat[c2], m)
                    lost[c] = add(lost[c], m)
                    nfirst[c] += 1
        e1 = [jnp.exp(sv1[i] - sv1[0]) for i in range(PEER_TOPK)]
        e2 = [jnp.exp(sv2[j] - sv2[0]) for j in range(PEER_TOPK)]
        nsel = [None] * PEER_TOPK
        zsum = None
        for c in _CELLS:
            base = float((c[0] + 1) * (c[1] + 1) - 1 + nfirst[c])
            cnt = base
            if beat[c] is not None:
                cnt = cnt + beat[c]
            if lost[c] is not None:
                cnt = cnt - lost[c]
            if isinstance(cnt, float):
                sel = jnp.full((PEER_HEADS, LANE), 1.0 if cnt < PEER_TOPK else 0.0, F32)
            else:
                sel = jnp.where(cnt < float(PEER_TOPK), 1.0, 0.0)
            nsel[c[0]] = add(nsel[c[0]], sel)
            zsum = add(zsum, sel * (e1[c[0]] * e2[c[1]]))
        for i in range(PEER_TOPK):
            nz_ref[0, i, :, lsl] = nsel[i]
            nz_ref[1, i, :, lsl] = e1[i] / zsum

    for h in range(PEER_HEADS):
        for lt in range(nl):
            lsl = slice(lt * LANE, (lt + 1) * LANE)
            rank1 = rk1_ref[h, :, lsl]
            nrow = jnp.zeros((PEER_NKEYS, LANE), F32)
            crow = jnp.zeros((PEER_NKEYS, LANE), F32)
            for i in range(PEER_TOPK):
                hit = rank1 == float(i)
                nrow = jnp.where(hit, nz_ref[0, i, h:h + 1, lsl], nrow)
                crow = jnp.where(hit, nz_ref[1, i, h:h + 1, lsl], crow)
            n1_ref[h, :, lsl] = nrow
            c1_ref[h, :, lsl] = crow


def _peer_topk(q, sub_keys, tt):
    t = q.shape[0]
    hk = (PEER_HEADS, PEER_NKEYS, t)
    blk = pl.BlockSpec((PEER_HEADS, PEER_NKEYS, tt), lambda i: (0, 0, i))
    return pl.pallas_call(
        functools.partial(_topk_body, tt=tt),
        grid=(t // tt,),
        in_specs=[pl.BlockSpec((tt, 2 * PEER_HEADS * PEER_HALF), lambda i: (i, 0)),
                  pl.BlockSpec((PEER_HEADS, 2, PEER_NKEYS, PEER_HALF), lambda i: (0, 0, 0, 0))],
        out_specs=[blk, blk, blk, blk],
        out_shape=[jax.ShapeDtypeStruct(hk, BF16), jax.ShapeDtypeStruct(hk, BF16),
                   jax.ShapeDtypeStruct(hk, F32), jax.ShapeDtypeStruct(hk, F32)],
        scratch_shapes=[pltpu.VMEM((2, PEER_HEADS, PEER_NKEYS, tt), F32),
                        pltpu.VMEM((2, PEER_TOPK, PEER_HEADS, tt), F32),
                        pltpu.VMEM((2, PEER_TOPK, PEER_HEADS, tt), F32),
                        pltpu.VMEM((PEER_HEADS, PEER_NKEYS, tt), F32)],
        compiler_params=_cparams(("parallel",)),
        name="peer_topk",
    )(q, sub_keys.astype(BF16))


PACK = 16


def _peer_mix_body(xn_ref, u_ref, vt_ref, r2_ref, e2_ref, n1_ref, c1_ref, res_ref, o_ref,
                   acc_ref, act_ref, w_ref, nc_ref, *, tt, eb, c1rows, c3rows):
    k = pl.program_id(1)
    nl = tt // LANE
    rep = PEER_NKEYS // PACK

    @pl.when(k == 0)
    def _():
        acc_ref[...] = jnp.zeros_like(acc_ref)

    for c3 in range(eb // c3rows):
        for c1 in range(c3rows // c1rows):
            r0 = c3 * c3rows + c1 * c1rows
            act_ref[r0:r0 + c1rows, :] = lax.dot_general(
                u_ref[r0:r0 + c1rows, :], xn_ref[...], (((1,), (1,)), ((), ())),
                preferred_element_type=F32)
            for al in range(r0 // PEER_NKEYS, (r0 + c1rows) // PEER_NKEYS):
                rows = slice(al * PEER_NKEYS, (al + 1) * PEER_NKEYS)
                for h in range(PEER_HEADS):
                    nc_ref[0, h, al] = jnp.broadcast_to(n1_ref[h, al:al + 1, :], (PACK, tt)).astype(BF16)
                    nc_ref[1, h, al] = jnp.broadcast_to(c1_ref[h, al:al + 1, :], (PACK, tt)).astype(BF16)
                for lt in range(nl):
                    lsl = slice(lt * LANE, (lt + 1) * LANE)
                    gate = None
                    for h in range(PEER_HEADS):
                        nb = jnp.concatenate([nc_ref[0, h, al, :, lsl]] * rep, axis=0)
                        cb = jnp.concatenate([nc_ref[1, h, al, :, lsl]] * rep, axis=0)
                        sel = jnp.minimum(jnp.maximum(nb - r2_ref[h, :, lsl], 0.0), cb)
                        term = sel * e2_ref[h, :, lsl]
                        gate = term if gate is None else gate + term
                    w_ref[rows, lsl] = _gelu_tanh(act_ref[rows, lsl]).astype(BF16) * gate
        q0 = c3 * c3rows
        acc_ref[...] += jnp.dot(vt_ref[:, q0:q0 + c3rows], w_ref[q0:q0 + c3rows, :],
                                preferred_element_type=F32)

    @pl.when(k == pl.num_programs(1) - 1)
    def _():
        o_ref[...] = res_ref[...] + acc_ref[...].T


def _peer_mix(xn, u16, vt16, r2, e2, n1, c1, res, tt, eb):
    t, d = res.shape
    nblk = u16.shape[0] // eb
    na = eb // PEER_NKEYS
    rank_blk = pl.BlockSpec((PEER_HEADS, PEER_NKEYS, tt), lambda i, k: (0, 0, i))
    row_blk = pl.BlockSpec((PEER_HEADS, na, tt), lambda i, k: (0, k, i))
    return pl.pallas_call(
        functools.partial(_peer_mix_body, tt=tt, eb=eb, c1rows=256, c3rows=512),
        grid=(t // tt, nblk),
        in_specs=[pl.BlockSpec((tt, d), lambda i, k: (i, 0)),
                  pl.BlockSpec((eb, d), lambda i, k: (k, 0)),
                  pl.BlockSpec((d, eb), lambda i, k: (0, k)),
                  rank_blk, rank_blk, row_blk, row_blk,
                  pl.BlockSpec((tt, d), lambda i, k: (i, 0))],
        out_specs=pl.BlockSpec((tt, d), lambda i, k: (i, 0)),
        out_shape=jax.ShapeDtypeStruct((t, d), F32),
        scratch_shapes=[pltpu.VMEM((d, tt), F32),
                        pltpu.VMEM((eb, tt), F32),
                        pltpu.VMEM((eb, tt), BF16),
                        pltpu.VMEM((2, PEER_HEADS, na, PACK, tt), BF16)],
        compiler_params=_cparams(("parallel", "arbitrary")),
        name="peer_mix",
    )(xn, u16, vt16, r2, e2, n1, c1, res)


def _norm_body(x_ref, g_ref, o_ref):
    x = x_ref[...]
    ms = jnp.mean(x * x, axis=-1, keepdims=True)
    o_ref[...] = x * lax.rsqrt(ms + NORM_EPS) * g_ref[...]


def _rmsnorm(x, g, tm):
    t, d = x.shape
    return pl.pallas_call(
        _norm_body,
        grid=(t // tm,),
        in_specs=[pl.BlockSpec((tm, d), lambda i: (i, 0)), pl.BlockSpec((1, d), lambda i: (0, 0))],
        out_specs=pl.BlockSpec((tm, d), lambda i: (i, 0)),
        out_shape=jax.ShapeDtypeStruct((t, d), F32),
        compiler_params=_cparams(("parallel",)),
        name="final_norm",
    )(x, g.reshape(1, d).astype(F32))


def _tiles(bsz, seq):
    t = bsz * seq
    pick = lambda n, opts: next(o for o in opts if n % o == 0)
    return dict(tm=pick(t, (256, 128)), ts=pick(seq, (512, 256, 128)),
                tk=pick(t, (256, 128)), tt=pick(t, (512, 256, 128)), eb=1024)


def kernel(x, even_norm_g, even_w_in, lru_conv_w, lru_conv_b, lru_gate_a_w, lru_gate_a_b, lru_gate_x_w,
           lru_gate_x_b, lru_lambda, ssd_conv_w, ssd_conv_b, ssd_dt_bias, ssd_a_log, ssd_d, ssd_norm_g,
           even_w_out, odd_norm_g, odd_w_in, odd_conv_w, odd_w_out, ffn_norm_g, peer_w_query,
           peer_sub_keys, peer_u, peer_v, final_norm_g):
    bsz, seq, d = x.shape
    assert d == D_MODEL and seq % SSD_CHUNK == 0
    depth = ffn_norm_g.shape[0]
    cfg = _tiles(bsz, seq)
    h = x.reshape(bsz * seq, d)
    for layer in range(depth):
        i = layer // 2
        if layer % 2 == 0:
            w_in = jnp.pad(even_w_in[i].astype(BF16), ((0, 0), (0, EVEN_PAD - even_w_in.shape[2])))
            proj = _norm_matmul(h, even_norm_g[i], w_in, F32, cfg["tm"])
            ya = _lru_branch(proj, bsz, seq, lru_conv_w[i], lru_conv_b[i], lru_gate_a_w[i], lru_gate_a_b[i],
                             lru_gate_x_w[i], lru_gate_x_b[i], lru_lambda[i], cfg["ts"])
            yb = _ssd_branch(proj, bsz, seq, ssd_conv_w[i], ssd_conv_b[i], ssd_dt_bias[i], ssd_a_log[i],
                             ssd_d[i], ssd_norm_g[i])
            h = _even_out(ya, yb, even_w_out[i], h, cfg["tm"])
        else:
            proj = _norm_matmul(h, odd_norm_g[i], odd_w_in[i].astype(BF16), F32, cfg["tm"])
            h = _odd_mix_out(proj, bsz, seq, odd_conv_w[i], odd_w_out[i], h, cfg["ts"])
        q, xn = _norm_matmul(h, ffn_norm_g[layer], peer_w_query[layer].astype(BF16), BF16, cfg["tm"],
                             want_xn=True)
        r2, e2, n1, c1 = _peer_topk(q, peer_sub_keys[layer], cfg["tk"])
        h = _peer_mix(xn, peer_u[layer].astype(BF16), peer_v[layer].astype(BF16).T, r2, e2, n1, c1, h,
                      cfg["tt"], cfg["eb"])
    return _rmsnorm(h, final_norm_g, cfg["tm"]).reshape(bsz, seq, d)
```

```python
import functools

import jax
import jax.numpy as jnp
from jax import lax
from jax.experimental import pallas as pl
from jax.experimental.pallas import tpu as pltpu

F32 = jnp.float32
BF16 = jnp.bfloat16

D_MODEL = 1024
NORM_EPS = 1e-6

LRU_WIDTH = 1024
LRU_HEADS = 8
LRU_HEAD_DIM = 128
LRU_C = 8.0

SSD_INNER = 1024
SSD_HEAD_DIM = 64
SSD_HEADS = 16
SSD_GROUPS = 2
SSD_HPG = 8
SSD_STATE = 128
SSD_CHUNK = 128
SSD_CONV_DIM = SSD_INNER + 2 * SSD_GROUPS * SSD_STATE

EVEN_MAIN = 2 * LRU_WIDTH + SSD_INNER + SSD_CONV_DIM
EVEN_PAD = EVEN_MAIN + 128

PEER_HEADS = 8
PEER_NKEYS = 128
PEER_NEXPERTS = PEER_NKEYS * PEER_NKEYS
PEER_TOPK = 16
PEER_HALF = 128

LANE = 128
SUBLANE = 8
VMEM_LIMIT = 56 * 1024 * 1024

NOT_TOP = 100.0

_CELLS = [(i, j) for i in range(PEER_TOPK) for j in range(PEER_TOPK) if (i + 1) * (j + 1) <= PEER_TOPK]


def _cparams(sem):
    return pltpu.CompilerParams(dimension_semantics=sem, vmem_limit_bytes=VMEM_LIMIT)


def _split_bf16(x, terms):
    parts = []
    rem = x
    for _ in range(terms):
        p = rem.astype(BF16)
        parts.append(p)
        rem = rem - p.astype(F32)
    return parts


def _dot_exact_rhs01(x, m01, terms):
    acc = None
    for p in _split_bf16(x, terms):
        d = jnp.dot(p, m01, preferred_element_type=F32)
        acc = d if acc is None else acc + d
    return acc


def _dot_exact_lhs01(m01, x, terms):
    acc = None
    for p in _split_bf16(x, terms):
        d = jnp.dot(m01, p, preferred_element_type=F32)
        acc = d if acc is None else acc + d
    return acc


def _softplus(x):
    return jnp.maximum(x, 0.0) + jnp.log1p(jnp.exp(-jnp.abs(x)))


def _expm1(x):
    u = jnp.exp(x)
    um1 = u - 1.0
    return jnp.where(um1 == 0.0, x, um1 * x / jnp.log(jnp.where(um1 == 0.0, 2.0, u)))


def _sigmoid(x):
    return 1.0 / (1.0 + jnp.exp(-x))


def _silu(x):
    return x * _sigmoid(x)


def _gelu_tanh(x):
    c = 0.7978845608028654
    hx = 0.5 * x
    return hx + hx * jnp.tanh(x * (c + (c * 0.044715) * (x * x)))


def _nmm_body(x_ref, g_ref, w_ref, o_ref, *xn_out):
    x = x_ref[...]
    ms = jnp.mean(x * x, axis=-1, keepdims=True)
    xn = (x * lax.rsqrt(ms + NORM_EPS) * g_ref[...]).astype(BF16)
    o_ref[...] = jnp.dot(xn, w_ref[...], preferred_element_type=F32).astype(o_ref.dtype)
    if xn_out:
        xn_out[0][...] = xn


def _norm_matmul(x, g, w_bf16, out_dtype, tm, want_xn=False):
    t, d = x.shape
    n = w_bf16.shape[1]
    out_shape = [jax.ShapeDtypeStruct((t, n), out_dtype)]
    out_specs = [pl.BlockSpec((tm, n), lambda i: (i, 0))]
    if want_xn:
        out_shape.append(jax.ShapeDtypeStruct((t, d), BF16))
        out_specs.append(pl.BlockSpec((tm, d), lambda i: (i, 0)))
    res = pl.pallas_call(
        _nmm_body,
        grid=(t // tm,),
        in_specs=[pl.BlockSpec((tm, d), lambda i: (i, 0)),
                  pl.BlockSpec((1, d), lambda i: (0, 0)),
                  pl.BlockSpec((d, n), lambda i: (0, 0))],
        out_specs=out_specs,
        out_shape=out_shape,
        compiler_params=_cparams(("parallel",)),
        name="norm_matmul",
    )(x, g.reshape(1, d), w_bf16)
    return res if want_xn else res[0]


def _lru_body(gate_ref, x_ref, cw_ref, cb_ref, gaw_ref, gab_ref, gxw_ref, gxb_ref, lam_ref,
              o_ref, xpad_ref, a_ref, b_ref, carry_ref, *, ts):
    w = LRU_WIDTH

    @pl.when(pl.program_id(1) == 0)
    def _():
        xpad_ref[0:SUBLANE, :] = jnp.zeros((SUBLANE, w), F32)
        carry_ref[...] = jnp.zeros((SUBLANE, w), F32)

    xpad_ref[SUBLANE:SUBLANE + ts, :] = x_ref[...]
    xa = cb_ref[...] + cw_ref[3:4, :] * x_ref[...]
    for k in range(3):
        xa = xa + cw_ref[k:k + 1, :] * xpad_ref[5 + k:5 + k + ts, :]
    xpad_ref[0:SUBLANE, :] = x_ref[ts - SUBLANE:ts, :]

    sp = _softplus(-lam_ref[...])
    for hd in range(LRU_HEADS):
        sl = slice(hd * LRU_HEAD_DIM, (hd + 1) * LRU_HEAD_DIM)
        xh = xa[:, sl]
        xh16 = xh.astype(BF16)
        r = _sigmoid(jnp.dot(xh16, gaw_ref[hd], preferred_element_type=F32) + gab_ref[:, sl])
        ig = _sigmoid(jnp.dot(xh16, gxw_ref[hd], preferred_element_type=F32) + gxb_ref[:, sl])
        log_a = (-LRU_C) * r * sp[:, sl]
        a_ref[:, sl] = jnp.exp(log_a)
        mult = jnp.sqrt(jnp.maximum(-_expm1(2.0 * log_a), 0.0))
        b_ref[:, sl] = mult * (ig * xh)

    row = lax.broadcasted_iota(jnp.int32, (SUBLANE, w), 0)

    def step(i, carry):
        r0 = pl.multiple_of(i * SUBLANE, SUBLANE)
        a = a_ref[pl.ds(r0, SUBLANE), :]
        b = b_ref[pl.ds(r0, SUBLANE), :]
        for d in (1, 2, 4):
            keep = row >= d
            a_sh = jnp.where(keep, pltpu.roll(a, d, 0), 1.0)
            b_sh = jnp.where(keep, pltpu.roll(b, d, 0), 0.0)
            b = a * b_sh + b
            a = a * a_sh
        hcur = b + a * carry
        b_ref[pl.ds(r0, SUBLANE), :] = hcur
        return jnp.broadcast_to(hcur[SUBLANE - 1:SUBLANE, :], (SUBLANE, w))

    carry = lax.fori_loop(0, ts // SUBLANE, step, carry_ref[...], unroll=2)
    carry_ref[...] = carry
    o_ref[...] = (_gelu_tanh(gate_ref[...]) * b_ref[...]).astype(o_ref.dtype)


def _lru_branch(proj, bsz, seq, cw, cb, gaw, gab, gxw, gxb, lam, ts):
    t = proj.shape[0]
    w = LRU_WIDTH
    nt = seq // ts
    vec = lambda v: v.reshape(1, w).astype(F32)
    full = lambda shape: pl.BlockSpec(shape, lambda b, s: (0,) * len(shape))
    return pl.pallas_call(
        functools.partial(_lru_body, ts=ts),
        grid=(bsz, nt),
        in_specs=[pl.BlockSpec((ts, w), lambda b, s: (b * nt + s, 0)),
                  pl.BlockSpec((ts, w), lambda b, s: (b * nt + s, 1)),
                  full((4, w)), full((1, w)),
                  full((LRU_HEADS, LRU_HEAD_DIM, LRU_HEAD_DIM)), full((1, w)),
                  full((LRU_HEADS, LRU_HEAD_DIM, LRU_HEAD_DIM)), full((1, w)),
                  full((1, w))],
        out_specs=pl.BlockSpec((ts, w), lambda b, s: (b * nt + s, 0)),
        out_shape=jax.ShapeDtypeStruct((t, w), BF16),
        scratch_shapes=[pltpu.VMEM((SUBLANE + ts, w), F32),
                        pltpu.VMEM((ts, w), F32),
                        pltpu.VMEM((ts, w), F32),
                        pltpu.VMEM((SUBLANE, w), F32)],
        compiler_params=_cparams(("parallel", "arbitrary")),
        name="rg_lru",
    )(proj, proj, cw.astype(F32), vec(cb), gaw.astype(BF16), vec(gab), gxw.astype(BF16), vec(gxb), vec(lam))


def _ssd_body(z_ref, xbc_ref, dt_ref, cw_ref, cb_ref, dtb_ref, alog_ref, dexp_ref, ng_ref, eexp_ref,
              o_ref, pad_ref, st_ref):
    L = SSD_CHUNK

    @pl.when(pl.program_id(1) == 0)
    def _():
        pad_ref[0:SUBLANE, :] = jnp.zeros((SUBLANE, SSD_CONV_DIM), F32)
        st_ref[...] = jnp.zeros((SSD_STATE, SSD_INNER), F32)

    pad_ref[SUBLANE:SUBLANE + L, :] = xbc_ref[...]
    conv = cb_ref[...] + cw_ref[3:4, :] * xbc_ref[...]
    for k in range(3):
        conv = conv + cw_ref[k:k + 1, :] * pad_ref[5 + k:5 + k + L, :]
    pad_ref[0:SUBLANE, :] = xbc_ref[L - SUBLANE:L, :]
    xbc = _silu(conv)
    xs = xbc[:, :SSD_INNER]

    dt = _softplus(dt_ref[...] + dtb_ref[...])
    da = dt * (-jnp.exp(alog_ref[...]))
    rr = lax.broadcasted_iota(jnp.int32, (L, L), 0)
    cc = lax.broadcasted_iota(jnp.int32, (L, L), 1)
    causal = rr >= cc
    tril = jnp.where(causal, 1.0, 0.0).astype(BF16)
    cum = _dot_exact_lhs01(tril, da, 3)
    cum_t = cum.T
    cum_last = cum[L - 1:L, :]
    w_end = jnp.exp(cum_last - cum) * dt
    ecum = jnp.exp(cum)
    stacked = jnp.concatenate([dt, w_end, ecum], axis=0)
    expd = _dot_exact_rhs01(stacked, eexp_ref[...], 2)
    dt_e = expd[0:L]
    wend_e = expd[L:2 * L]
    ecum_e = expd[2 * L:3 * L]
    xdt = xs * dt_e
    xw = (xs * wend_e).astype(BF16)

    lane = lax.broadcasted_iota(jnp.int32, (L, LANE), 1)
    left = lane < SSD_HEAD_DIM
    y_parts = []
    for g in range(SSD_GROUPS):
        bg = xbc[:, SSD_INNER + g * SSD_STATE:SSD_INNER + (g + 1) * SSD_STATE]
        cg = xbc[:, SSD_INNER + SSD_GROUPS * SSD_STATE + g * SSD_STATE:
                 SSD_INNER + SSD_GROUPS * SSD_STATE + (g + 1) * SSD_STATE]
        bg16 = bg.astype(BF16)
        cg16 = cg.astype(BF16)
        cb = lax.dot_general(cg16, bg16, (((1,), (1,)), ((), ())), preferred_element_type=F32)
        for pair in range(SSD_HPG // 2):
            tile = g * (SSD_HPG // 2) + pair
            xp = xdt[:, tile * LANE:(tile + 1) * LANE]
            acc = None
            for side in range(2):
                j = 2 * tile + side
                seg = cum[:, j:j + 1] - cum_t[j:j + 1, :]
                decay = jnp.where(causal, jnp.exp(jnp.minimum(seg, 0.0)), 0.0)
                m = (cb * decay).astype(BF16)
                xsel = jnp.where(left if side == 0 else jnp.logical_not(left), xp, 0.0).astype(BF16)
                d = jnp.dot(m, xsel, preferred_element_type=F32)
                acc = d if acc is None else acc + d
            y_parts.append(acc)
    y_diag = jnp.concatenate(y_parts, axis=1)

    half = SSD_HPG * SSD_HEAD_DIM
    y_off_parts = []
    for g in range(SSD_GROUPS):
        bg = xbc[:, SSD_INNER + g * SSD_STATE:SSD_INNER + (g + 1) * SSD_STATE]
        cg = xbc[:, SSD_INNER + SSD_GROUPS * SSD_STATE + g * SSD_STATE:
                 SSD_INNER + SSD_GROUPS * SSD_STATE + (g + 1) * SSD_STATE]
        gs = slice(g * half, (g + 1) * half)
        prev = st_ref[:, gs]
        y_off_parts.append(jnp.dot(cg.astype(BF16), prev.astype(BF16), preferred_element_type=F32))
        new_states = jnp.dot(bg.T.astype(BF16), xw[:, gs], preferred_element_type=F32)
        st_ref[:, gs] = prev * ecum_e[L - 1:L, gs] + new_states
    y_off = jnp.concatenate(y_off_parts, axis=1) * ecum_e

    y = y_diag + y_off + dexp_ref[...] * xs
    yz = y * _silu(z_ref[...])
    ms = jnp.mean(yz * yz, axis=-1, keepdims=True)
    o_ref[...] = (yz * lax.rsqrt(ms + NORM_EPS) * ng_ref[...]).astype(o_ref.dtype)


def _ssd_branch(proj, bsz, seq, cw, cb, dt_bias, a_log, d_skip, norm_g):
    t = proj.shape[0]
    L = SSD_CHUNK
    nc = seq // L
    pad16 = lambda v: jnp.zeros((1, LANE), F32).at[0, :SSD_HEADS].set(v.astype(F32))
    d_exp = jnp.repeat(d_skip.astype(F32), SSD_HEAD_DIM).reshape(1, SSD_INNER)
    e_exp = (jnp.arange(LANE)[:, None] == (jnp.arange(SSD_INNER)[None, :] // SSD_HEAD_DIM)).astype(BF16)
    full = lambda shape: pl.BlockSpec(shape, lambda b, c: (0,) * len(shape))
    return pl.pallas_call(
        _ssd_body,
        grid=(bsz, nc),
        in_specs=[pl.BlockSpec((L, SSD_INNER), lambda b, c: (b * nc + c, 2)),
                  pl.BlockSpec((L, SSD_CONV_DIM), lambda b, c: (b * nc + c, 2)),
                  pl.BlockSpec((L, LANE), lambda b, c: (b * nc + c, EVEN_MAIN // LANE)),
                  full((4, SSD_CONV_DIM)), full((1, SSD_CONV_DIM)),
                  full((1, LANE)), full((1, LANE)), full((1, SSD_INNER)), full((1, SSD_INNER)),
                  full((LANE, SSD_INNER))],
        out_specs=pl.BlockSpec((L, SSD_INNER), lambda b, c: (b * nc + c, 0)),
        out_shape=jax.ShapeDtypeStruct((t, SSD_INNER), BF16),
        scratch_shapes=[pltpu.VMEM((SUBLANE + L, SSD_CONV_DIM), F32),
                        pltpu.VMEM((SSD_STATE, SSD_INNER), F32)],
        compiler_params=_cparams(("parallel", "arbitrary")),
        name="ssd",
    )(proj, proj, proj, cw.astype(F32), cb.reshape(1, -1).astype(F32), pad16(dt_bias), pad16(a_log),
      d_exp, norm_g.reshape(1, -1).astype(F32), e_exp)


def _even_out_body(ya_ref, yb_ref, wa_ref, wb_ref, res_ref, o_ref):
    o_ref[...] = (res_ref[...]
                  + jnp.dot(ya_ref[...], wa_ref[...], preferred_element_type=F32)
                  + jnp.dot(yb_ref[...], wb_ref[...], preferred_element_type=F32))


def _even_out(ya, yb, w_out, res, tm):
    t, d = res.shape
    w16 = w_out.astype(BF16)
    row = lambda n: pl.BlockSpec((tm, n), lambda i: (i, 0))
    return pl.pallas_call(
        _even_out_body,
        grid=(t // tm,),
        in_specs=[row(LRU_WIDTH), row(SSD_INNER),
                  pl.BlockSpec((LRU_WIDTH, d), lambda i: (0, 0)),
                  pl.BlockSpec((SSD_INNER, d), lambda i: (0, 0)),
                  row(d)],
        out_specs=row(d),
        out_shape=jax.ShapeDtypeStruct((t, d), F32),
        compiler_params=_cparams(("parallel",)),
        name="even_out",
    )(ya, yb, w16[:LRU_WIDTH], w16[LRU_WIDTH:], res)


def _odd_body(bg_ref, cg_ref, v_ref, cw_ref, w_ref, res_ref, o_ref, pad_ref, *, ts):
    w = D_MODEL

    @pl.when(pl.program_id(1) == 0)
    def _():
        pad_ref[0:SUBLANE, :] = jnp.zeros((SUBLANE, w), F32)

    cv = cg_ref[...] * v_ref[...]
    pad_ref[SUBLANE:SUBLANE + ts, :] = cv
    conv = cw_ref[2:3, :] * cv
    for k in range(2):
        conv = conv + cw_ref[k:k + 1, :] * pad_ref[6 + k:6 + k + ts, :]
    pad_ref[0:SUBLANE, :] = cv[ts - SUBLANE:ts, :]
    u = (bg_ref[...] * conv).astype(BF16)
    o_ref[...] = res_ref[...] + jnp.dot(u, w_ref[...], preferred_element_type=F32)


def _odd_mix_out(proj, bsz, seq, cw, w_out, res, ts):
    t, d = res.shape
    nt = seq // ts
    col = lambda c: pl.BlockSpec((ts, d), lambda b, s: (b * nt + s, c))
    return pl.pallas_call(
        functools.partial(_odd_body, ts=ts),
        grid=(bsz, nt),
        in_specs=[col(0), col(1), col(2),
                  pl.BlockSpec((3, d), lambda b, s: (0, 0)),
                  pl.BlockSpec((d, d), lambda b, s: (0, 0)),
                  col(0)],
        out_specs=col(0),
        out_shape=jax.ShapeDtypeStruct((t, d), F32),
        scratch_shapes=[pltpu.VMEM((SUBLANE + ts, d), F32)],
        compiler_params=_cparams(("parallel", "arbitrary")),
        name="odd_mix_out",
    )(proj, proj, proj, cw.astype(F32), w_out.astype(BF16), res)


def _topk_body(q_ref, sk_ref, r2_ref, e2_ref, n1_ref, c1_ref, rk_ref, sv_ref, nz_ref, s2_ref, bad_ref, *, tt):
    nl = tt // LANE
    iota_n = lax.broadcasted_iota(jnp.int32, (PEER_NKEYS, LANE), 0).astype(F32)

    def scores(h, p):
        qp = q_ref[:, (2 * h + p) * PEER_HALF:(2 * h + p + 1) * PEER_HALF]
        return lax.dot_general(sk_ref[h, p], qp, (((1,), (1,)), ((), ())),
                               preferred_element_type=F32)

    bad_ref[...] = jnp.zeros_like(bad_ref)
    tag_floor = -(2.0 ** 126)

    def fast_top16(h, p):
        x = scores(h, p)
        if p == 1:
            s2_ref[h] = x
        low = jnp.min(x, axis=0, keepdims=True)
        for r in range(PEER_TOPK):
            slabs = [x[i * SUBLANE:(i + 1) * SUBLANE, :] for i in range(PEER_NKEYS // SUBLANE)]
            while len(slabs) > 1:
                slabs = [jnp.maximum(slabs[i], slabs[i + 1]) for i in range(0, len(slabs), 2)]
            m = jnp.max(slabs[0], axis=0, keepdims=True)
            sv_ref[p, r, h:h + 1, :] = m
            x = jnp.where(x == m, -(2.0 ** 121) * (64 - r), x)
        tagged = x <= tag_floor
        rk_ref[p, h] = jnp.where(tagged, 64.0 + x * (2.0 ** -121), NOT_TOP)
        cnt = jnp.sum(jnp.where(tagged, 1.0, 0.0), axis=0, keepdims=True)
        bad_ref[0:1, :] += jnp.abs(cnt - float(PEER_TOPK)) + jnp.where(low <= tag_floor, 1.0, 0.0)

    for h in range(PEER_HEADS):
        for p in range(2):
            fast_top16(h, p)

    @pl.when(jnp.max(bad_ref[0:1, :]) > 0.0)
    def _():
        def top16(x0, p, h, lsl):
            def rnd(r, carry):
                x, rank, rf = carry
                m = jnp.max(x, axis=0, keepdims=True)
                idx = jnp.min(jnp.where(x == m, iota_n, float(PEER_NKEYS)), axis=0, keepdims=True)
                hit = iota_n == idx
                sv_ref[p, r, h:h + 1, lsl] = m
                return jnp.where(hit, -jnp.inf, x), jnp.where(hit, rf, rank), rf + 1.0

            init = (x0, jnp.full((PEER_NKEYS, LANE), NOT_TOP, F32), jnp.zeros((1, LANE), F32))
            return lax.fori_loop(0, PEER_TOPK, rnd, init)[1]

        for h in range(PEER_HEADS):
            for p in range(2):
                s = scores(h, p)
                for lt in range(nl):
                    lsl = slice(lt * LANE, (lt + 1) * LANE)
                    rk_ref[p, h, :, lsl] = top16(s[:, lsl], p, h, lsl)

    rk1_ref = rk_ref.at[0]
    for h in range(PEER_HEADS):
        r2_ref[h] = rk_ref[1, h].astype(BF16)
        e2_ref[h] = jnp.exp(s2_ref[h] - sv_ref[1, 0, h:h + 1, :]).astype(BF16)

    for lt in range(nl):
        lsl = slice(lt * LANE, (lt + 1) * LANE)
        sv1 = [sv_ref[0, i, :, lsl] for i in range(PEER_TOPK)]
        sv2 = [sv_ref[1, j, :, lsl] for j in range(PEER_TOPK)]
        cand = {c: sv1[c[0]] + sv2[c[1]] for c in _CELLS}
        beat = {c: None for c in _CELLS}
        lost = {c: None for c in _CELLS}
        nfirst = {c: 0 for c in _CELLS}
        add = lambda a, b: b if a is None else a + b
        for ci, c in enumerate(_CELLS):
            for c2 in _CELLS[ci + 1:]:
                if c2[0] > c[0] and c2[1] < c[1]:
                    m = jnp.where(cand[c] >= cand[c2], 1.0, 0.0)
                    beat[c2] = add(beat[c2], m)
                    lost[c] = add(lost[c], m)
                    nfirst[c] += 1
        e1 = [jnp.exp(sv1[i] - sv1[0]) for i in range(PEER_TOPK)]
        e2 = [jnp.exp(sv2[j] - sv2[0]) for j in range(PEER_TOPK)]
        nsel = [None] * PEER_TOPK
        zsum = None
        for c in _CELLS:
            base = float((c[0] + 1) * (c[1] + 1) - 1 + nfirst[c])
            cnt = base
            if beat[c] is not None:
                cnt = cnt + beat[c]
            if lost[c] is not None:
                cnt = cnt - lost[c]
            if isinstance(cnt, float):
                sel = jnp.full((PEER_HEADS, LANE), 1.0 if cnt < PEER_TOPK else 0.0, F32)
            else:
                sel = jnp.where(cnt < float(PEER_TOPK), 1.0, 0.0)
            nsel[c[0]] = add(nsel[c[0]], sel)
            zsum = add(zsum, sel * (e1[c[0]] * e2[c[1]]))
        for i in range(PEER_TOPK):
            nz_ref[0, i, :, lsl] = nsel[i]
            nz_ref[1, i, :, lsl] = e1[i] / zsum

    for h in range(PEER_HEADS):
        for lt in range(nl):
            lsl = slice(lt * LANE, (lt + 1) * LANE)
            rank1 = rk1_ref[h, :, lsl]
            nrow = jnp.zeros((PEER_NKEYS, LANE), F32)
            crow = jnp.zeros((PEER_NKEYS, LANE), F32)
            for i in range(PEER_TOPK):
                hit = rank1 == float(i)
                nrow = jnp.where(hit, nz_ref[0, i, h:h + 1, lsl], nrow)
                crow = jnp.where(hit, nz_ref[1, i, h:h + 1, lsl], crow)
            n1_ref[h, :, lsl] = nrow
            c1_ref[h, :, lsl] = crow


def _peer_topk(q, sub_keys, tt):
    t = q.shape[0]
    hk = (PEER_HEADS, PEER_NKEYS, t)
    blk = pl.BlockSpec((PEER_HEADS, PEER_NKEYS, tt), lambda i: (0, 0, i))
    return pl.pallas_call(
        functools.partial(_topk_body, tt=tt),
        grid=(t // tt,),
        in_specs=[pl.BlockSpec((tt, 2 * PEER_HEADS * PEER_HALF), lambda i: (i, 0)),
                  pl.BlockSpec((PEER_HEADS, 2, PEER_NKEYS, PEER_HALF), lambda i: (0, 0, 0, 0))],
        out_specs=[blk, blk, blk, blk],
        out_shape=[jax.ShapeDtypeStruct(hk, BF16), jax.ShapeDtypeStruct(hk, BF16),
                   jax.ShapeDtypeStruct(hk, F32), jax.ShapeDtypeStruct(hk, F32)],
        scratch_shapes=[pltpu.VMEM((2, PEER_HEADS, PEER_NKEYS, tt), F32),
                        pltpu.VMEM((2, PEER_TOPK, PEER_HEADS, tt), F32),
                        pltpu.VMEM((2, PEER_TOPK, PEER_HEADS, tt), F32),
                        pltpu.VMEM((PEER_HEADS, PEER_NKEYS, tt), F32),
                        pltpu.VMEM((SUBLANE, tt), F32)],
        compiler_params=_cparams(("parallel",)),
        name="peer_topk",
    )(q, sub_keys.astype(BF16))


PACK = 16


def _peer_mix_body(xn_ref, u_ref, vt_ref, r2_ref, e2_ref, n1_ref, c1_ref, res_ref, o_ref,
                   acc_ref, act_ref, w_ref, nc_ref, *, tt, eb, c1rows, c3rows):
    k = pl.program_id(1)
    nl = tt // LANE
    rep = PEER_NKEYS // PACK

    @pl.when(k == 0)
    def _():
        acc_ref[...] = jnp.zeros_like(acc_ref)

    for c3 in range(eb // c3rows):
        for c1 in range(c3rows // c1rows):
            r0 = c3 * c3rows + c1 * c1rows
            act_ref[r0:r0 + c1rows, :] = lax.dot_general(
                u_ref[r0:r0 + c1rows, :], xn_ref[...], (((1,), (1,)), ((), ())),
                preferred_element_type=F32)
            for al in range(r0 // PEER_NKEYS, (r0 + c1rows) // PEER_NKEYS):
                rows = slice(al * PEER_NKEYS, (al + 1) * PEER_NKEYS)
                for h in range(PEER_HEADS):
                    nc_ref[0, h, al] = jnp.broadcast_to(n1_ref[h, al:al + 1, :], (PACK, tt)).astype(BF16)
                    nc_ref[1, h, al] = jnp.broadcast_to(c1_ref[h, al:al + 1, :], (PACK, tt)).astype(BF16)
                for lt in range(nl):
                    lsl = slice(lt * LANE, (lt + 1) * LANE)
                    gate = None
                    for h in range(PEER_HEADS):
                        nb = jnp.concatenate([nc_ref[0, h, al, :, lsl]] * rep, axis=0)
                        cb = jnp.concatenate([nc_ref[1, h, al, :, lsl]] * rep, axis=0)
                        sel = jnp.minimum(jnp.maximum(nb - r2_ref[h, :, lsl], 0.0), cb)
                        term = sel * e2_ref[h, :, lsl]
                        gate = term if gate is None else gate + term
                    w_ref[rows, lsl] = _gelu_tanh(act_ref[rows, lsl].astype(BF16)) * gate
        q0 = c3 * c3rows
        acc_ref[...] += jnp.dot(vt_ref[:, q0:q0 + c3rows], w_ref[q0:q0 + c3rows, :],
                                preferred_element_type=F32)

    @pl.when(k == pl.num_programs(1) - 1)
    def _():
        o_ref[...] = res_ref[...] + acc_ref[...].T


def _peer_mix(xn, u16, vt16, r2, e2, n1, c1, res, tt, eb, c1rows=512, c3rows=1024):
    t, d = res.shape
    nblk = u16.shape[0] // eb
    na = eb // PEER_NKEYS
    rank_blk = pl.BlockSpec((PEER_HEADS, PEER_NKEYS, tt), lambda i, k: (0, 0, i))
    row_blk = pl.BlockSpec((PEER_HEADS, na, tt), lambda i, k: (0, k, i))
    return pl.pallas_call(
        functools.partial(_peer_mix_body, tt=tt, eb=eb, c1rows=c1rows, c3rows=c3rows),
        grid=(t // tt, nblk),
        in_specs=[pl.BlockSpec((tt, d), lambda i, k: (i, 0)),
                  pl.BlockSpec((eb, d), lambda i, k: (k, 0)),
                  pl.BlockSpec((d, eb), lambda i, k: (0, k)),
                  rank_blk, rank_blk, row_blk, row_blk,
                  pl.BlockSpec((tt, d), lambda i, k: (i, 0))],
        out_specs=pl.BlockSpec((tt, d), lambda i, k: (i, 0)),
        out_shape=jax.ShapeDtypeStruct((t, d), F32),
        scratch_shapes=[pltpu.VMEM((d, tt), F32),
                        pltpu.VMEM((eb, tt), F32),
                        pltpu.VMEM((eb, tt), BF16),
                        pltpu.VMEM((2, PEER_HEADS, na, PACK, tt), BF16)],
        compiler_params=_cparams(("parallel", "arbitrary")),
        name="peer_mix",
    )(xn, u16, vt16, r2, e2, n1, c1, res)


def _norm_body(x_ref, g_ref, o_ref):
    x = x_ref[...]
    ms = jnp.mean(x * x, axis=-1, keepdims=True)
    o_ref[...] = x * lax.rsqrt(ms + NORM_EPS) * g_ref[...]


def _rmsnorm(x, g, tm):
    t, d = x.shape
    return pl.pallas_call(
        _norm_body,
        grid=(t // tm,),
        in_specs=[pl.BlockSpec((tm, d), lambda i: (i, 0)), pl.BlockSpec((1, d), lambda i: (0, 0))],
        out_specs=pl.BlockSpec((tm, d), lambda i: (i, 0)),
        out_shape=jax.ShapeDtypeStruct((t, d), F32),
        compiler_params=_cparams(("parallel",)),
        name="final_norm",
    )(x, g.reshape(1, d).astype(F32))


def _tiles(bsz, seq):
    t = bsz * seq
    pick = lambda n, opts: next(o for o in opts if n % o == 0)
    return dict(tm=pick(t, (256, 128)), ts=pick(seq, (512, 256, 128)),
                tk=pick(t, (256, 128)), tt=pick(t, (512, 256, 128)), eb=1024)


def kernel(x, even_norm_g, even_w_in, lru_conv_w, lru_conv_b, lru_gate_a_w, lru_gate_a_b, lru_gate_x_w,
           lru_gate_x_b, lru_lambda, ssd_conv_w, ssd_conv_b, ssd_dt_bias, ssd_a_log, ssd_d, ssd_norm_g,
           even_w_out, odd_norm_g, odd_w_in, odd_conv_w, odd_w_out, ffn_norm_g, peer_w_query,
           peer_sub_keys, peer_u, peer_v, final_norm_g):
    bsz, seq, d = x.shape
    assert d == D_MODEL and seq % SSD_CHUNK == 0
    depth = ffn_norm_g.shape[0]
    cfg = _tiles(bsz, seq)
    h = x.reshape(bsz * seq, d)
    for layer in range(depth):
        i = layer // 2
        if layer % 2 == 0:
            w_in = jnp.pad(even_w_in[i].astype(BF16), ((0, 0), (0, EVEN_PAD - even_w_in.shape[2])))
            proj = _norm_matmul(h, even_norm_g[i], w_in, F32, cfg["tm"])
            ya = _lru_branch(proj, bsz, seq, lru_conv_w[i], lru_conv_b[i], lru_gate_a_w[i], lru_gate_a_b[i],
                             lru_gate_x_w[i], lru_gate_x_b[i], lru_lambda[i], cfg["ts"])
            yb = _ssd_branch(proj, bsz, seq, ssd_conv_w[i], ssd_conv_b[i], ssd_dt_bias[i], ssd_a_log[i],
                             ssd_d[i], ssd_norm_g[i])
            h = _even_out(ya, yb, even_w_out[i], h, cfg["tm"])
        else:
            proj = _norm_matmul(h, odd_norm_g[i], odd_w_in[i].astype(BF16), F32, cfg["tm"])
            h = _odd_mix_out(proj, bsz, seq, odd_conv_w[i], odd_w_out[i], h, cfg["ts"])
        q, xn = _norm_matmul(h, ffn_norm_g[layer], peer_w_query[layer].astype(BF16), BF16, cfg["tm"],
                             want_xn=True)
        r2, e2, n1, c1 = _peer_topk(q, peer_sub_keys[layer], cfg["tk"])
        h = _peer_mix(xn, peer_u[layer].astype(BF16), peer_v[layer].astype(BF16).T, r2, e2, n1, c1, h,
                      cfg["tt"], cfg["eb"])
    return _rmsnorm(h, final_norm_g, cfg["tm"]).reshape(bsz, seq, d)
```

```python
import functools

import jax
import jax.numpy as jnp
from jax import lax
from jax.experimental import pallas as pl
from jax.experimental.pallas import tpu as pltpu

F32 = jnp.float32
BF16 = jnp.bfloat16

D_MODEL = 1024
NORM_EPS = 1e-6

LRU_WIDTH = 1024
LRU_HEADS = 8
LRU_HEAD_DIM = 128
LRU_C = 8.0

SSD_INNER = 1024
SSD_HEAD_DIM = 64
SSD_HEADS = 16
SSD_GROUPS = 2
SSD_HPG = 8
SSD_STATE = 128
SSD_CHUNK = 128
SSD_CONV_DIM = SSD_INNER + 2 * SSD_GROUPS * SSD_STATE

EVEN_MAIN = 2 * LRU_WIDTH + SSD_INNER + SSD_CONV_DIM
EVEN_PAD = EVEN_MAIN + 128

PEER_HEADS = 8
PEER_NKEYS = 128
PEER_NEXPERTS = PEER_NKEYS * PEER_NKEYS
PEER_TOPK = 16
PEER_HALF = 128

LANE = 128
SUBLANE = 8
VMEM_LIMIT = 56 * 1024 * 1024

NOT_TOP = 100.0

_CELLS = [(i, j) for i in range(PEER_TOPK) for j in range(PEER_TOPK) if (i + 1) * (j + 1) <= PEER_TOPK]


def _cparams(sem):
    return pltpu.CompilerParams(dimension_semantics=sem, vmem_limit_bytes=VMEM_LIMIT)


def _split_bf16(x, terms):
    parts = []
    rem = x
    for _ in range(terms):
        p = rem.astype(BF16)
        parts.append(p)
        rem = rem - p.astype(F32)
    return parts


def _dot_exact_rhs01(x, m01, terms):
    acc = None
    for p in _split_bf16(x, terms):
        d = jnp.dot(p, m01, preferred_element_type=F32)
        acc = d if acc is None else acc + d
    return acc


def _dot_exact_lhs01(m01, x, terms):
    acc = None
    for p in _split_bf16(x, terms):
        d = jnp.dot(m01, p, preferred_element_type=F32)
        acc = d if acc is None else acc + d
    return acc


def _softplus(x):
    return jnp.maximum(x, 0.0) + jnp.log1p(jnp.exp(-jnp.abs(x)))


def _expm1(x):
    u = jnp.exp(x)
    um1 = u - 1.0
    return jnp.where(um1 == 0.0, x, um1 * x / jnp.log(jnp.where(um1 == 0.0, 2.0, u)))


def _sigmoid(x):
    return 1.0 / (1.0 + jnp.exp(-x))


def _silu(x):
    return x * _sigmoid(x)


def _gelu_tanh(x):
    c = 0.7978845608028654
    hx = 0.5 * x
    return hx + hx * jnp.tanh(x * (c + (c * 0.044715) * (x * x)))


def _nmm_body(x_ref, g_ref, w_ref, o_ref, *xn_out):
    x = x_ref[...]
    ms = jnp.mean(x * x, axis=-1, keepdims=True)
    xn = (x * lax.rsqrt(ms + NORM_EPS) * g_ref[...]).astype(BF16)
    o_ref[...] = jnp.dot(xn, w_ref[...], preferred_element_type=F32).astype(o_ref.dtype)
    if xn_out:
        xn_out[0][...] = xn


def _norm_matmul(x, g, w_bf16, out_dtype, tm, want_xn=False):
    t, d = x.shape
    n = w_bf16.shape[1]
    out_shape = [jax.ShapeDtypeStruct((t, n), out_dtype)]
    out_specs = [pl.BlockSpec((tm, n), lambda i: (i, 0))]
    if want_xn:
        out_shape.append(jax.ShapeDtypeStruct((t, d), BF16))
        out_specs.append(pl.BlockSpec((tm, d), lambda i: (i, 0)))
    res = pl.pallas_call(
        _nmm_body,
        grid=(t // tm,),
        in_specs=[pl.BlockSpec((tm, d), lambda i: (i, 0)),
                  pl.BlockSpec((1, d), lambda i: (0, 0)),
                  pl.BlockSpec((d, n), lambda i: (0, 0))],
        out_specs=out_specs,
        out_shape=out_shape,
        compiler_params=_cparams(("parallel",)),
        name="norm_matmul",
    )(x, g.reshape(1, d), w_bf16)
    return res if want_xn else res[0]


def _lru_body(gate_ref, x_ref, cw_ref, cb_ref, gaw_ref, gab_ref, gxw_ref, gxb_ref, lam_ref,
              o_ref, xpad_ref, a_ref, b_ref, carry_ref, *, ts):
    w = LRU_WIDTH

    @pl.when(pl.program_id(1) == 0)
    def _():
        xpad_ref[0:SUBLANE, :] = jnp.zeros((SUBLANE, w), F32)
        carry_ref[...] = jnp.zeros((SUBLANE, w), F32)

    xpad_ref[SUBLANE:SUBLANE + ts, :] = x_ref[...]
    xa = cb_ref[...] + cw_ref[3:4, :] * x_ref[...]
    for k in range(3):
        xa = xa + cw_ref[k:k + 1, :] * xpad_ref[5 + k:5 + k + ts, :]
    xpad_ref[0:SUBLANE, :] = x_ref[ts - SUBLANE:ts, :]

    sp = _softplus(-lam_ref[...])
    for hd in range(LRU_HEADS):
        sl = slice(hd * LRU_HEAD_DIM, (hd + 1) * LRU_HEAD_DIM)
        xh = xa[:, sl]
        xh16 = xh.astype(BF16)
        r = _sigmoid(jnp.dot(xh16, gaw_ref[hd], preferred_element_type=F32) + gab_ref[:, sl])
        ig = _sigmoid(jnp.dot(xh16, gxw_ref[hd], preferred_element_type=F32) + gxb_ref[:, sl])
        log_a = (-LRU_C) * r * sp[:, sl]
        a_ref[:, sl] = jnp.exp(log_a)
        mult = jnp.sqrt(jnp.maximum(-_expm1(2.0 * log_a), 0.0))
        b_ref[:, sl] = mult * (ig * xh)

    row = lax.broadcasted_iota(jnp.int32, (SUBLANE, w), 0)

    def step(i, carry):
        r0 = pl.multiple_of(i * SUBLANE, SUBLANE)
        a = a_ref[pl.ds(r0, SUBLANE), :]
        b = b_ref[pl.ds(r0, SUBLANE), :]
        for d in (1, 2, 4):
            keep = row >= d
            a_sh = jnp.where(keep, pltpu.roll(a, d, 0), 1.0)
            b_sh = jnp.where(keep, pltpu.roll(b, d, 0), 0.0)
            b = a * b_sh + b
            a = a * a_sh
        hcur = b + a * carry
        b_ref[pl.ds(r0, SUBLANE), :] = hcur
        return jnp.broadcast_to(hcur[SUBLANE - 1:SUBLANE, :], (SUBLANE, w))

    carry = lax.fori_loop(0, ts // SUBLANE, step, carry_ref[...], unroll=2)
    carry_ref[...] = carry
    o_ref[...] = (_gelu_tanh(gate_ref[...]) * b_ref[...]).astype(o_ref.dtype)


def _lru_branch(proj, bsz, seq, cw, cb, gaw, gab, gxw, gxb, lam, ts):
    t = proj.shape[0]
    w = LRU_WIDTH
    nt = seq // ts
    vec = lambda v: v.reshape(1, w).astype(F32)
    full = lambda shape: pl.BlockSpec(shape, lambda b, s: (0,) * len(shape))
    return pl.pallas_call(
        functools.partial(_lru_body, ts=ts),
        grid=(bsz, nt),
        in_specs=[pl.BlockSpec((ts, w), lambda b, s: (b * nt + s, 0)),
                  pl.BlockSpec((ts, w), lambda b, s: (b * nt + s, 1)),
                  full((4, w)), full((1, w)),
                  full((LRU_HEADS, LRU_HEAD_DIM, LRU_HEAD_DIM)), full((1, w)),
                  full((LRU_HEADS, LRU_HEAD_DIM, LRU_HEAD_DIM)), full((1, w)),
                  full((1, w))],
        out_specs=pl.BlockSpec((ts, w), lambda b, s: (b * nt + s, 0)),
        out_shape=jax.ShapeDtypeStruct((t, w), BF16),
        scratch_shapes=[pltpu.VMEM((SUBLANE + ts, w), F32),
                        pltpu.VMEM((ts, w), F32),
                        pltpu.VMEM((ts, w), F32),
                        pltpu.VMEM((SUBLANE, w), F32)],
        compiler_params=_cparams(("parallel", "arbitrary")),
        name="rg_lru",
    )(proj, proj, cw.astype(F32), vec(cb), gaw.astype(BF16), vec(gab), gxw.astype(BF16), vec(gxb), vec(lam))


def _ssd_body(z_ref, xbc_ref, dt_ref, cw_ref, cb_ref, dtb_ref, alog_ref, dexp_ref, ng_ref, eexp_ref,
              o_ref, pad_ref, st_ref):
    L = SSD_CHUNK

    @pl.when(pl.program_id(1) == 0)
    def _():
        pad_ref[0:SUBLANE, :] = jnp.zeros((SUBLANE, SSD_CONV_DIM), F32)
        st_ref[...] = jnp.zeros((SSD_STATE, SSD_INNER), F32)

    pad_ref[SUBLANE:SUBLANE + L, :] = xbc_ref[...]
    conv = cb_ref[...] + cw_ref[3:4, :] * xbc_ref[...]
    for k in range(3):
        conv = conv + cw_ref[k:k + 1, :] * pad_ref[5 + k:5 + k + L, :]
    pad_ref[0:SUBLANE, :] = xbc_ref[L - SUBLANE:L, :]
    xbc = _silu(conv)
    xs = xbc[:, :SSD_INNER]

    dt = _softplus(dt_ref[...] + dtb_ref[...])
    da = dt * (-jnp.exp(alog_ref[...]))
    rr = lax.broadcasted_iota(jnp.int32, (L, L), 0)
    cc = lax.broadcasted_iota(jnp.int32, (L, L), 1)
    causal = rr >= cc
    tril = jnp.where(causal, 1.0, 0.0).astype(BF16)
    cum = _dot_exact_lhs01(tril, da, 3)
    cum_t = cum.T
    cum_last = cum[L - 1:L, :]
    w_end = jnp.exp(cum_last - cum) * dt
    ecum = jnp.exp(cum)
    stacked = jnp.concatenate([dt, w_end, ecum], axis=0)
    expd = _dot_exact_rhs01(stacked, eexp_ref[...], 2)
    dt_e = expd[0:L]
    wend_e = expd[L:2 * L]
    ecum_e = expd[2 * L:3 * L]
    xdt = xs * dt_e
    xw = (xs * wend_e).astype(BF16)

    lane = lax.broadcasted_iota(jnp.int32, (L, LANE), 1)
    left = lane < SSD_HEAD_DIM
    y_parts = []
    for g in range(SSD_GROUPS):
        bg = xbc[:, SSD_INNER + g * SSD_STATE:SSD_INNER + (g + 1) * SSD_STATE]
        cg = xbc[:, SSD_INNER + SSD_GROUPS * SSD_STATE + g * SSD_STATE:
                 SSD_INNER + SSD_GROUPS * SSD_STATE + (g + 1) * SSD_STATE]
        bg16 = bg.astype(BF16)
        cg16 = cg.astype(BF16)
        cb = lax.dot_general(cg16, bg16, (((1,), (1,)), ((), ())), preferred_element_type=F32)
        for pair in range(SSD_HPG // 2):
            tile = g * (SSD_HPG // 2) + pair
            xp = xdt[:, tile * LANE:(tile + 1) * LANE]
            acc = None
            for side in range(2):
                j = 2 * tile + side
                seg = cum[:, j:j + 1] - cum_t[j:j + 1, :]
                decay = jnp.where(causal, jnp.exp(jnp.minimum(seg, 0.0)), 0.0)
                m = (cb * decay).astype(BF16)
                xsel = jnp.where(left if side == 0 else jnp.logical_not(left), xp, 0.0).astype(BF16)
                d = jnp.dot(m, xsel, preferred_element_type=F32)
                acc = d if acc is None else acc + d
            y_parts.append(acc)
    y_diag = jnp.concatenate(y_parts, axis=1)

    half = SSD_HPG * SSD_HEAD_DIM
    y_off_parts = []
    for g in range(SSD_GROUPS):
        bg = xbc[:, SSD_INNER + g * SSD_STATE:SSD_INNER + (g + 1) * SSD_STATE]
        cg = xbc[:, SSD_INNER + SSD_GROUPS * SSD_STATE + g * SSD_STATE:
                 SSD_INNER + SSD_GROUPS * SSD_STATE + (g + 1) * SSD_STATE]
        gs = slice(g * half, (g + 1) * half)
        prev = st_ref[:, gs]
        y_off_parts.append(jnp.dot(cg.astype(BF16), prev.astype(BF16), preferred_element_type=F32))
        new_states = jnp.dot(bg.T.astype(BF16), xw[:, gs], preferred_element_type=F32)
        st_ref[:, gs] = prev * ecum_e[L - 1:L, gs] + new_states
    y_off = jnp.concatenate(y_off_parts, axis=1) * ecum_e

    y = y_diag + y_off + dexp_ref[...] * xs
    yz = y * _silu(z_ref[...])
    ms = jnp.mean(yz * yz, axis=-1, keepdims=True)
    o_ref[...] = (yz * lax.rsqrt(ms + NORM_EPS) * ng_ref[...]).astype(o_ref.dtype)


def _ssd_branch(proj, bsz, seq, cw, cb, dt_bias, a_log, d_skip, norm_g):
    t = proj.shape[0]
    L = SSD_CHUNK
    nc = seq // L
    pad16 = lambda v: jnp.zeros((1, LANE), F32).at[0, :SSD_HEADS].set(v.astype(F32))
    d_exp = jnp.repeat(d_skip.astype(F32), SSD_HEAD_DIM).reshape(1, SSD_INNER)
    e_exp = (jnp.arange(LANE)[:, None] == (jnp.arange(SSD_INNER)[None, :] // SSD_HEAD_DIM)).astype(BF16)
    full = lambda shape: pl.BlockSpec(shape, lambda b, c: (0,) * len(shape))
    return pl.pallas_call(
        _ssd_body,
        grid=(bsz, nc),
        in_specs=[pl.BlockSpec((L, SSD_INNER), lambda b, c: (b * nc + c, 2)),
                  pl.BlockSpec((L, SSD_CONV_DIM), lambda b, c: (b * nc + c, 2)),
                  pl.BlockSpec((L, LANE), lambda b, c: (b * nc + c, EVEN_MAIN // LANE)),
                  full((4, SSD_CONV_DIM)), full((1, SSD_CONV_DIM)),
                  full((1, LANE)), full((1, LANE)), full((1, SSD_INNER)), full((1, SSD_INNER)),
                  full((LANE, SSD_INNER))],
        out_specs=pl.BlockSpec((L, SSD_INNER), lambda b, c: (b * nc + c, 0)),
        out_shape=jax.ShapeDtypeStruct((t, SSD_INNER), BF16),
        scratch_shapes=[pltpu.VMEM((SUBLANE + L, SSD_CONV_DIM), F32),
                        pltpu.VMEM((SSD_STATE, SSD_INNER), F32)],
        compiler_params=_cparams(("parallel", "arbitrary")),
        name="ssd",
    )(proj, proj, proj, cw.astype(F32), cb.reshape(1, -1).astype(F32), pad16(dt_bias), pad16(a_log),
      d_exp, norm_g.reshape(1, -1).astype(F32), e_exp)


def _even_out_body(ya_ref, yb_ref, wa_ref, wb_ref, res_ref, o_ref):
    o_ref[...] = (res_ref[...]
                  + jnp.dot(ya_ref[...], wa_ref[...], preferred_element_type=F32)
                  + jnp.dot(yb_ref[...], wb_ref[...], preferred_element_type=F32))


def _even_out(ya, yb, w_out, res, tm):
    t, d = res.shape
    w16 = w_out.astype(BF16)
    row = lambda n: pl.BlockSpec((tm, n), lambda i: (i, 0))
    return pl.pallas_call(
        _even_out_body,
        grid=(t // tm,),
        in_specs=[row(LRU_WIDTH), row(SSD_INNER),
                  pl.BlockSpec((LRU_WIDTH, d), lambda i: (0, 0)),
                  pl.BlockSpec((SSD_INNER, d), lambda i: (0, 0)),
                  row(d)],
        out_specs=row(d),
        out_shape=jax.ShapeDtypeStruct((t, d), F32),
        compiler_params=_cparams(("parallel",)),
        name="even_out",
    )(ya, yb, w16[:LRU_WIDTH], w16[LRU_WIDTH:], res)


def _odd_body(bg_ref, cg_ref, v_ref, cw_ref, w_ref, res_ref, o_ref, pad_ref, *, ts):
    w = D_MODEL

    @pl.when(pl.program_id(1) == 0)
    def _():
        pad_ref[0:SUBLANE, :] = jnp.zeros((SUBLANE, w), F32)

    cv = cg_ref[...] * v_ref[...]
    pad_ref[SUBLANE:SUBLANE + ts, :] = cv
    conv = cw_ref[2:3, :] * cv
    for k in range(2):
        conv = conv + cw_ref[k:k + 1, :] * pad_ref[6 + k:6 + k + ts, :]
    pad_ref[0:SUBLANE, :] = cv[ts - SUBLANE:ts, :]
    u = (bg_ref[...] * conv).astype(BF16)
    o_ref[...] = res_ref[...] + jnp.dot(u, w_ref[...], preferred_element_type=F32)


def _odd_mix_out(proj, bsz, seq, cw, w_out, res, ts):
    t, d = res.shape
    nt = seq // ts
    col = lambda c: pl.BlockSpec((ts, d), lambda b, s: (b * nt + s, c))
    return pl.pallas_call(
        functools.partial(_odd_body, ts=ts),
        grid=(bsz, nt),
        in_specs=[col(0), col(1), col(2),
                  pl.BlockSpec((3, d), lambda b, s: (0, 0)),
                  pl.BlockSpec((d, d), lambda b, s: (0, 0)),
                  col(0)],
        out_specs=col(0),
        out_shape=jax.ShapeDtypeStruct((t, d), F32),
        scratch_shapes=[pltpu.VMEM((SUBLANE + ts, d), F32)],
        compiler_params=_cparams(("parallel", "arbitrary")),
        name="odd_mix_out",
    )(proj, proj, proj, cw.astype(F32), w_out.astype(BF16), res)


def _topk_body(q_ref, sk_ref, r2_ref, e2_ref, n1_ref, c1_ref, rk_ref, sv_ref, nz_ref, s2_ref, bad_ref, *, tt):
    nl = tt // LANE
    iota_n = lax.broadcasted_iota(jnp.int32, (PEER_NKEYS, LANE), 0).astype(F32)

    def scores(h, p):
        qp = q_ref[:, (2 * h + p) * PEER_HALF:(2 * h + p + 1) * PEER_HALF]
        return lax.dot_general(sk_ref[h, p], qp, (((1,), (1,)), ((), ())),
                               preferred_element_type=F32)

    bad_ref[...] = jnp.zeros_like(bad_ref)
    tag_floor = -(2.0 ** 126)

    def fast_top16(h, p):
        x = scores(h, p)
        if p == 1:
            s2_ref[h] = x
        low = jnp.min(x, axis=0, keepdims=True)
        for r in range(PEER_TOPK):
            slabs = [x[i * SUBLANE:(i + 1) * SUBLANE, :] for i in range(PEER_NKEYS // SUBLANE)]
            while len(slabs) > 1:
                slabs = [jnp.maximum(slabs[i], slabs[i + 1]) for i in range(0, len(slabs), 2)]
            m = jnp.max(slabs[0], axis=0, keepdims=True)
            sv_ref[p, r, h:h + 1, :] = m
            x = jnp.where(x == m, -(2.0 ** 121) * (64 - r), x)
        tagged = x <= tag_floor
        rk_ref[p, h] = jnp.where(tagged, 64.0 + x * (2.0 ** -121), NOT_TOP)
        cnt = jnp.sum(jnp.where(tagged, 1.0, 0.0), axis=0, keepdims=True)
        bad_ref[0:1, :] += jnp.abs(cnt - float(PEER_TOPK)) + jnp.where(low <= tag_floor, 1.0, 0.0)

    for h in range(PEER_HEADS):
        for p in range(2):
            fast_top16(h, p)

    @pl.when(jnp.max(bad_ref[0:1, :]) > 0.0)
    def _():
        def top16(x0, p, h, lsl):
            def rnd(r, carry):
                x, rank, rf = carry
                m = jnp.max(x, axis=0, keepdims=True)
                idx = jnp.min(jnp.where(x == m, iota_n, float(PEER_NKEYS)), axis=0, keepdims=True)
                hit = iota_n == idx
                sv_ref[p, r, h:h + 1, lsl] = m
                return jnp.where(hit, -jnp.inf, x), jnp.where(hit, rf, rank), rf + 1.0

            init = (x0, jnp.full((PEER_NKEYS, LANE), NOT_TOP, F32), jnp.zeros((1, LANE), F32))
            return lax.fori_loop(0, PEER_TOPK, rnd, init)[1]

        for h in range(PEER_HEADS):
            for p in range(2):
                s = scores(h, p)
                for lt in range(nl):
                    lsl = slice(lt * LANE, (lt + 1) * LANE)
                    rk_ref[p, h, :, lsl] = top16(s[:, lsl], p, h, lsl)

    rk1_ref = rk_ref.at[0]
    for h in range(PEER_HEADS):
        r2_ref[h] = rk_ref[1, h].astype(BF16)
        e2_ref[h] = jnp.exp(s2_ref[h] - sv_ref[1, 0, h:h + 1, :]).astype(BF16)

    for lt in range(nl):
        lsl = slice(lt * LANE, (lt + 1) * LANE)
        sv1 = [sv_ref[0, i, :, lsl] for i in range(PEER_TOPK)]
        sv2 = [sv_ref[1, j, :, lsl] for j in range(PEER_TOPK)]
        cand = {c: sv1[c[0]] + sv2[c[1]] for c in _CELLS}
        beat = {c: None for c in _CELLS}
        lost = {c: None for c in _CELLS}
        nfirst = {c: 0 for c in _CELLS}
        add = lambda a, b: b if a is None else a + b
        for ci, c in enumerate(_CELLS):
            for c2 in _CELLS[ci + 1:]:
                if c2[0] > c[0] and c2[1] < c[1]:
                    m = jnp.where(cand[c] >= cand[c2], 1.0, 0.0)
                    beat[c2] = add(beat[c2], m)
                    lost[c] = add(lost[c], m)
                    nfirst[c] += 1
        e1 = [jnp.exp(sv1[i] - sv1[0]) for i in range(PEER_TOPK)]
        e2 = [jnp.exp(sv2[j] - sv2[0]) for j in range(PEER_TOPK)]
        nsel = [None] * PEER_TOPK
        zsum = None
        for c in _CELLS:
            base = float((c[0] + 1) * (c[1] + 1) - 1 + nfirst[c])
            cnt = base
            if beat[c] is not None:
                cnt = cnt + beat[c]
            if lost[c] is not None:
                cnt = cnt - lost[c]
            if isinstance(cnt, float):
                sel = jnp.full((PEER_HEADS, LANE), 1.0 if cnt < PEER_TOPK else 0.0, F32)
            else:
                sel = jnp.where(cnt < float(PEER_TOPK), 1.0, 0.0)
            nsel[c[0]] = add(nsel[c[0]], sel)
            zsum = add(zsum, sel * (e1[c[0]] * e2[c[1]]))
        for i in range(PEER_TOPK):
            nz_ref[0, i, :, lsl] = nsel[i]
            nz_ref[1, i, :, lsl] = e1[i] / zsum

    for h in range(PEER_HEADS):
        for lt in range(nl):
            lsl = slice(lt * LANE, (lt + 1) * LANE)
            rank1 = rk1_ref[h, :, lsl]
            nrow = jnp.zeros((PEER_NKEYS, LANE), F32)
            crow = jnp.zeros((PEER_NKEYS, LANE), F32)
            for i in range(PEER_TOPK):
                hit = rank1 == float(i)
                nrow = jnp.where(hit, nz_ref[0, i, h:h + 1, lsl], nrow)
                crow = jnp.where(hit, nz_ref[1, i, h:h + 1, lsl], crow)
            n1_ref[h, :, lsl] = nrow
            c1_ref[h, :, lsl] = crow


def _peer_topk(q, sub_keys, tt):
    t = q.shape[0]
    hk = (t // tt, PEER_HEADS, PEER_NKEYS, tt)
    blk = pl.BlockSpec((None, PEER_HEADS, PEER_NKEYS, tt), lambda i: (i, 0, 0, 0))
    return pl.pallas_call(
        functools.partial(_topk_body, tt=tt),
        grid=(t // tt,),
        in_specs=[pl.BlockSpec((tt, 2 * PEER_HEADS * PEER_HALF), lambda i: (i, 0)),
                  pl.BlockSpec((PEER_HEADS, 2, PEER_NKEYS, PEER_HALF), lambda i: (0, 0, 0, 0))],
        out_specs=[blk, blk, blk, blk],
        out_shape=[jax.ShapeDtypeStruct(hk, BF16), jax.ShapeDtypeStruct(hk, BF16),
                   jax.ShapeDtypeStruct(hk, F32), jax.ShapeDtypeStruct(hk, F32)],
        scratch_shapes=[pltpu.VMEM((2, PEER_HEADS, PEER_NKEYS, tt), F32),
                        pltpu.VMEM((2, PEER_TOPK, PEER_HEADS, tt), F32),
                        pltpu.VMEM((2, PEER_TOPK, PEER_HEADS, tt), F32),
                        pltpu.VMEM((PEER_HEADS, PEER_NKEYS, tt), F32),
                        pltpu.VMEM((SUBLANE, tt), F32)],
        compiler_params=_cparams(("parallel",)),
        name="peer_topk",
    )(q, sub_keys.astype(BF16))


PACK = 16


def _peer_mix_body(xn_ref, u_ref, vt_ref, r2_ref, e2_ref, n1_ref, c1_ref, res_ref, o_ref,
                   acc_ref, act_ref, w_ref, nc_ref, *, tt, eb, c1rows, c3rows):
    k = pl.program_id(1)
    nl = tt // LANE
    tk = r2_ref.shape[-1]
    rep = PEER_NKEYS // PACK

    @pl.when(k == 0)
    def _():
        acc_ref[...] = jnp.zeros_like(acc_ref)

    for c3 in range(eb // c3rows):
        for c1 in range(c3rows // c1rows):
            r0 = c3 * c3rows + c1 * c1rows
            act_ref[r0:r0 + c1rows, :] = lax.dot_general(
                u_ref[r0:r0 + c1rows, :], xn_ref[...], (((1,), (1,)), ((), ())),
                preferred_element_type=F32)
            for al in range(r0 // PEER_NKEYS, (r0 + c1rows) // PEER_NKEYS):
                rows = slice(al * PEER_NKEYS, (al + 1) * PEER_NKEYS)
                for h in range(PEER_HEADS):
                    for j in range(tt // tk):
                        jsl = slice(j * tk, (j + 1) * tk)
                        nc_ref[0, h, al, :, jsl] = jnp.broadcast_to(n1_ref[j, h, al:al + 1, :], (PACK, tk)).astype(BF16)
                        nc_ref[1, h, al, :, jsl] = jnp.broadcast_to(c1_ref[j, h, al:al + 1, :], (PACK, tk)).astype(BF16)
                for lt in range(nl):
                    lsl = slice(lt * LANE, (lt + 1) * LANE)
                    j, jl = divmod(lt * LANE, tk)
                    tsl = slice(jl, jl + LANE)
                    gate = None
                    for h in range(PEER_HEADS):
                        nb = jnp.concatenate([nc_ref[0, h, al, :, lsl]] * rep, axis=0)
                        cb = jnp.concatenate([nc_ref[1, h, al, :, lsl]] * rep, axis=0)
                        sel = jnp.minimum(jnp.maximum(nb - r2_ref[j, h, :, tsl], 0.0), cb)
                        term = sel * e2_ref[j, h, :, tsl]
                        gate = term if gate is None else gate + term
                    w_ref[rows, lsl] = _gelu_tanh(act_ref[rows, lsl].astype(BF16)) * gate
        q0 = c3 * c3rows
        acc_ref[...] += jnp.dot(vt_ref[:, q0:q0 + c3rows], w_ref[q0:q0 + c3rows, :],
                                preferred_element_type=F32)

    @pl.when(k == pl.num_programs(1) - 1)
    def _():
        o_ref[...] = res_ref[...] + acc_ref[...].T


def _peer_mix(xn, u16, vt16, r2, e2, n1, c1, res, tt, eb, c1rows=512, c3rows=1024):
    t, d = res.shape
    nblk = u16.shape[0] // eb
    na = eb // PEER_NKEYS
    tk = r2.shape[-1]
    assert tt % tk == 0
    vt_blocks = vt16.reshape(d, nblk, eb).transpose(1, 0, 2)
    rank_blk = pl.BlockSpec((tt // tk, PEER_HEADS, PEER_NKEYS, tk), lambda i, k: (i, 0, 0, 0))
    row_blk = pl.BlockSpec((tt // tk, PEER_HEADS, na, tk), lambda i, k: (i, 0, k, 0))
    return pl.pallas_call(
        functools.partial(_peer_mix_body, tt=tt, eb=eb, c1rows=c1rows, c3rows=c3rows),
        grid=(t // tt, nblk),
        in_specs=[pl.BlockSpec((tt, d), lambda i, k: (i, 0)),
                  pl.BlockSpec((eb, d), lambda i, k: (k, 0)),
                  pl.BlockSpec((None, d, eb), lambda i, k: (k, 0, 0)),
                  rank_blk, rank_blk, row_blk, row_blk,
                  pl.BlockSpec((tt, d), lambda i, k: (i, 0))],
        out_specs=pl.BlockSpec((tt, d), lambda i, k: (i, 0)),
        out_shape=jax.ShapeDtypeStruct((t, d), F32),
        scratch_shapes=[pltpu.VMEM((d, tt), F32),
                        pltpu.VMEM((eb, tt), F32),
                        pltpu.VMEM((eb, tt), BF16),
                        pltpu.VMEM((2, PEER_HEADS, na, PACK, tt), BF16)],
        compiler_params=_cparams(("parallel", "arbitrary")),
        name="peer_mix",
    )(xn, u16, vt_blocks, r2, e2, n1, c1, res)


def _norm_body(x_ref, g_ref, o_ref):
    x = x_ref[...]
    ms = jnp.mean(x * x, axis=-1, keepdims=True)
    o_ref[...] = x * lax.rsqrt(ms + NORM_EPS) * g_ref[...]


def _rmsnorm(x, g, tm):
    t, d = x.shape
    return pl.pallas_call(
        _norm_body,
        grid=(t // tm,),
        in_specs=[pl.BlockSpec((tm, d), lambda i: (i, 0)), pl.BlockSpec((1, d), lambda i: (0, 0))],
        out_specs=pl.BlockSpec((tm, d), lambda i: (i, 0)),
        out_shape=jax.ShapeDtypeStruct((t, d), F32),
        compiler_params=_cparams(("parallel",)),
        name="final_norm",
    )(x, g.reshape(1, d).astype(F32))


def _tiles(bsz, seq):
    t = bsz * seq
    pick = lambda n, opts: next(o for o in opts if n % o == 0)
    return dict(tm=pick(t, (256, 128)), ts=pick(seq, (512, 256, 128)),
                tk=pick(t, (512, 256, 128)), tt=pick(t, (512, 256, 128)), eb=1024)


def kernel(x, even_norm_g, even_w_in, lru_conv_w, lru_conv_b, lru_gate_a_w, lru_gate_a_b, lru_gate_x_w,
           lru_gate_x_b, lru_lambda, ssd_conv_w, ssd_conv_b, ssd_dt_bias, ssd_a_log, ssd_d, ssd_norm_g,
           even_w_out, odd_norm_g, odd_w_in, odd_conv_w, odd_w_out, ffn_norm_g, peer_w_query,
           peer_sub_keys, peer_u, peer_v, final_norm_g):
    bsz, seq, d = x.shape
    assert d == D_MODEL and seq % SSD_CHUNK == 0
    depth = ffn_norm_g.shape[0]
    cfg = _tiles(bsz, seq)
    h = x.reshape(bsz * seq, d)
    for layer in range(depth):
        i = layer // 2
        if layer % 2 == 0:
            w_in = jnp.pad(even_w_in[i].astype(BF16), ((0, 0), (0, EVEN_PAD - even_w_in.shape[2])))
            proj = _norm_matmul(h, even_norm_g[i], w_in, F32, cfg["tm"])
            ya = _lru_branch(proj, bsz, seq, lru_conv_w[i], lru_conv_b[i], lru_gate_a_w[i], lru_gate_a_b[i],
                             lru_gate_x_w[i], lru_gate_x_b[i], lru_lambda[i], cfg["ts"])
            yb = _ssd_branch(proj, bsz, seq, ssd_conv_w[i], ssd_conv_b[i], ssd_dt_bias[i], ssd_a_log[i],
                             ssd_d[i], ssd_norm_g[i])
            h = _even_out(ya, yb, even_w_out[i], h, cfg["tm"])
        else:
            proj = _norm_matmul(h, odd_norm_g[i], odd_w_in[i].astype(BF16), F32, cfg["tm"])
            h = _odd_mix_out(proj, bsz, seq, odd_conv_w[i], odd_w_out[i], h, cfg["ts"])
        q, xn = _norm_matmul(h, ffn_norm_g[layer], peer_w_query[layer].astype(BF16), BF16, cfg["tm"],
                             want_xn=True)
        r2, e2, n1, c1 = _peer_topk(q, peer_sub_keys[layer], cfg["tk"])
        h = _peer_mix(xn, peer_u[layer].astype(BF16), peer_v[layer].astype(BF16).T, r2, e2, n1, c1, h,
                      cfg["tt"], cfg["eb"])
    return _rmsnorm(h, final_norm_g, cfg["tm"]).reshape(bsz, seq, d)
```

```python
import functools

import jax
import jax.numpy as jnp
from jax import lax
from jax.experimental import pallas as pl
from jax.experimental.pallas import tpu as pltpu

F32 = jnp.float32
BF16 = jnp.bfloat16

D_MODEL = 1024
NORM_EPS = 1e-6

LRU_WIDTH = 1024
LRU_HEADS = 8
LRU_HEAD_DIM = 128
LRU_C = 8.0

SSD_INNER = 1024
SSD_HEAD_DIM = 64
SSD_HEADS = 16
SSD_GROUPS = 2
SSD_HPG = 8
SSD_STATE = 128
SSD_CHUNK = 128
SSD_CONV_DIM = SSD_INNER + 2 * SSD_GROUPS * SSD_STATE

EVEN_MAIN = 2 * LRU_WIDTH + SSD_INNER + SSD_CONV_DIM
EVEN_PAD = EVEN_MAIN + 128

PEER_HEADS = 8
PEER_NKEYS = 128
PEER_NEXPERTS = PEER_NKEYS * PEER_NKEYS
PEER_TOPK = 16
PEER_HALF = 128

LANE = 128
SUBLANE = 8
VMEM_LIMIT = 56 * 1024 * 1024

NOT_TOP = 100.0

_CELLS = [(i, j) for i in range(PEER_TOPK) for j in range(PEER_TOPK) if (i + 1) * (j + 1) <= PEER_TOPK]


def _cparams(sem):
    return pltpu.CompilerParams(dimension_semantics=sem, vmem_limit_bytes=VMEM_LIMIT)


def _split_bf16(x, terms):
    parts = []
    rem = x
    for _ in range(terms):
        p = rem.astype(BF16)
        parts.append(p)
        rem = rem - p.astype(F32)
    return parts


def _dot_exact_rhs01(x, m01, terms):
    acc = None
    for p in _split_bf16(x, terms):
        d = jnp.dot(p, m01, preferred_element_type=F32)
        acc = d if acc is None else acc + d
    return acc


def _dot_exact_lhs01(m01, x, terms):
    acc = None
    for p in _split_bf16(x, terms):
        d = jnp.dot(m01, p, preferred_element_type=F32)
        acc = d if acc is None else acc + d
    return acc


def _softplus(x):
    return jnp.maximum(x, 0.0) + jnp.log1p(jnp.exp(-jnp.abs(x)))


def _expm1(x):
    u = jnp.exp(x)
    um1 = u - 1.0
    return jnp.where(um1 == 0.0, x, um1 * x / jnp.log(jnp.where(um1 == 0.0, 2.0, u)))


def _sigmoid(x):
    return 0.5 + 0.5 * jnp.tanh(0.5 * x)


def _silu(x):
    return x * _sigmoid(x)


def _gelu_tanh(x):
    c = 0.7978845608028654
    hx = 0.5 * x
    return hx + hx * jnp.tanh(x * (c + (c * 0.044715) * (x * x)))


def _nmm_body(x_ref, g_ref, w_ref, o_ref, *extra, tail, want_xn):
    x = x_ref[...]
    ms = jnp.mean(x * x, axis=-1, keepdims=True)
    xn = (x * lax.rsqrt(ms + NORM_EPS) * g_ref[...]).astype(BF16)
    y = jnp.dot(xn, w_ref[...], preferred_element_type=F32)
    n = y.shape[1] - tail
    o_ref[...] = y[:, :n].astype(o_ref.dtype)
    extra = list(extra)
    if tail:
        extra.pop(0)[...] = y[:, n:]
    if want_xn:
        extra.pop(0)[...] = xn


def _norm_matmul(x, g, w_bf16, out_dtype, tm, want_xn=False, tail=0):
    t, d = x.shape
    n = w_bf16.shape[1]
    out_shape = [jax.ShapeDtypeStruct((t, n - tail), out_dtype)]
    out_specs = [pl.BlockSpec((tm, n - tail), lambda i: (i, 0))]
    if tail:
        out_shape.append(jax.ShapeDtypeStruct((t, tail), F32))
        out_specs.append(pl.BlockSpec((tm, tail), lambda i: (i, 0)))
    if want_xn:
        out_shape.append(jax.ShapeDtypeStruct((t, d), BF16))
        out_specs.append(pl.BlockSpec((tm, d), lambda i: (i, 0)))
    res = pl.pallas_call(
        functools.partial(_nmm_body, tail=tail, want_xn=want_xn),
        grid=(t // tm,),
        in_specs=[pl.BlockSpec((tm, d), lambda i: (i, 0)),
                  pl.BlockSpec((1, d), lambda i: (0, 0)),
                  pl.BlockSpec((d, n), lambda i: (0, 0))],
        out_specs=out_specs,
        out_shape=out_shape,
        compiler_params=_cparams(("parallel",)),
        name="norm_matmul",
    )(x, g.reshape(1, d), w_bf16)
    return res if (want_xn or tail) else res[0]


def _lru_body(gate_ref, x_ref, cw_ref, cb_ref, gaw_ref, gab_ref, gxw_ref, gxb_ref, lam_ref,
              o_ref, xpad_ref, a_ref, b_ref, carry_ref, *, ts):
    w = LRU_WIDTH

    @pl.when(pl.program_id(1) == 0)
    def _():
        xpad_ref[0:SUBLANE, :] = jnp.zeros((SUBLANE, w), F32)
        carry_ref[...] = jnp.zeros((SUBLANE, w), F32)

    xpad_ref[SUBLANE:SUBLANE + ts, :] = x_ref[...].astype(F32)
    xa = cb_ref[...] + cw_ref[3:4, :] * xpad_ref[SUBLANE:SUBLANE + ts, :]
    for k in range(3):
        xa = xa + cw_ref[k:k + 1, :] * xpad_ref[5 + k:5 + k + ts, :]
    xpad_ref[0:SUBLANE, :] = xpad_ref[ts:ts + SUBLANE, :]

    sp = _softplus(-lam_ref[...])
    for hd in range(LRU_HEADS):
        sl = slice(hd * LRU_HEAD_DIM, (hd + 1) * LRU_HEAD_DIM)
        xh = xa[:, sl]
        xh16 = xh.astype(BF16)
        r = _sigmoid(jnp.dot(xh16, gaw_ref[hd], preferred_element_type=F32) + gab_ref[:, sl])
        ig = _sigmoid(jnp.dot(xh16, gxw_ref[hd], preferred_element_type=F32) + gxb_ref[:, sl])
        log_a = (-LRU_C) * r * sp[:, sl]
        a_ref[:, sl] = jnp.exp(log_a)
        mult = jnp.sqrt(jnp.maximum(-_expm1(2.0 * log_a), 0.0))
        b_ref[:, sl] = mult * (ig * xh)

    row = lax.broadcasted_iota(jnp.int32, (SUBLANE, w), 0)

    def step(i, carry):
        r0 = pl.multiple_of(i * SUBLANE, SUBLANE)
        a = a_ref[pl.ds(r0, SUBLANE), :]
        b = b_ref[pl.ds(r0, SUBLANE), :]
        for d in (1, 2, 4):
            keep = row >= d
            a_sh = jnp.where(keep, pltpu.roll(a, d, 0), 1.0)
            b_sh = jnp.where(keep, pltpu.roll(b, d, 0), 0.0)
            b = a * b_sh + b
            a = a * a_sh
        hcur = b + a * carry
        b_ref[pl.ds(r0, SUBLANE), :] = hcur
        return jnp.broadcast_to(hcur[SUBLANE - 1:SUBLANE, :], (SUBLANE, w))

    carry = lax.fori_loop(0, ts // SUBLANE, step, carry_ref[...], unroll=2)
    carry_ref[...] = carry
    o_ref[...] = (_gelu_tanh(gate_ref[...].astype(F32)) * b_ref[...]).astype(o_ref.dtype)


def _lru_branch(proj, bsz, seq, cw, cb, gaw, gab, gxw, gxb, lam, ts):
    t = proj.shape[0]
    w = LRU_WIDTH
    nt = seq // ts
    vec = lambda v: v.reshape(1, w).astype(F32)
    full = lambda shape: pl.BlockSpec(shape, lambda b, s: (0,) * len(shape))
    return pl.pallas_call(
        functools.partial(_lru_body, ts=ts),
        grid=(bsz, nt),
        in_specs=[pl.BlockSpec((ts, w), lambda b, s: (b * nt + s, 0)),
                  pl.BlockSpec((ts, w), lambda b, s: (b * nt + s, 1)),
                  full((4, w)), full((1, w)),
                  full((LRU_HEADS, LRU_HEAD_DIM, LRU_HEAD_DIM)), full((1, w)),
                  full((LRU_HEADS, LRU_HEAD_DIM, LRU_HEAD_DIM)), full((1, w)),
                  full((1, w))],
        out_specs=pl.BlockSpec((ts, w), lambda b, s: (b * nt + s, 0)),
        out_shape=jax.ShapeDtypeStruct((t, w), BF16),
        scratch_shapes=[pltpu.VMEM((SUBLANE + ts, w), F32),
                        pltpu.VMEM((ts, w), F32),
                        pltpu.VMEM((ts, w), F32),
                        pltpu.VMEM((SUBLANE, w), F32)],
        compiler_params=_cparams(("parallel", "arbitrary")),
        name="rg_lru",
    )(proj, proj, cw.astype(F32), vec(cb), gaw.astype(BF16), vec(gab), gxw.astype(BF16), vec(gxb), vec(lam))


def _ssd_body(z_ref, xbc_ref, dt_ref, cw_ref, cb_ref, dtb_ref, alog_ref, dexp_ref, ng_ref, eexp_ref,
              o_ref, pad_ref, st_ref):
    L = SSD_CHUNK

    @pl.when(pl.program_id(1) == 0)
    def _():
        pad_ref[0:SUBLANE, :] = jnp.zeros((SUBLANE, SSD_CONV_DIM), F32)
        st_ref[...] = jnp.zeros((SSD_STATE, SSD_INNER), F32)

    pad_ref[SUBLANE:SUBLANE + L, :] = xbc_ref[...].astype(F32)
    conv = cb_ref[...] + cw_ref[3:4, :] * pad_ref[SUBLANE:SUBLANE + L, :]
    for k in range(3):
        conv = conv + cw_ref[k:k + 1, :] * pad_ref[5 + k:5 + k + L, :]
    pad_ref[0:SUBLANE, :] = pad_ref[L:L + SUBLANE, :]
    xbc = _silu(conv)
    xs = xbc[:, :SSD_INNER]

    dt = _softplus(dt_ref[...] + dtb_ref[...])
    da = dt * (-jnp.exp(alog_ref[...]))
    rr = lax.broadcasted_iota(jnp.int32, (L, L), 0)
    cc = lax.broadcasted_iota(jnp.int32, (L, L), 1)
    causal = rr >= cc
    tril = jnp.where(causal, 1.0, 0.0).astype(BF16)
    cum = _dot_exact_lhs01(tril, da, 3)
    cum_t = cum.T
    cum_last = cum[L - 1:L, :]
    w_end = jnp.exp(cum_last - cum) * dt
    ecum = jnp.exp(cum)
    stacked = jnp.concatenate([dt, w_end, ecum], axis=0)
    expd = _dot_exact_rhs01(stacked, eexp_ref[...], 2)
    dt_e = expd[0:L]
    wend_e = expd[L:2 * L]
    ecum_e = expd[2 * L:3 * L]
    xdt = xs * dt_e
    xw = (xs * wend_e).astype(BF16)

    lane = lax.broadcasted_iota(jnp.int32, (L, LANE), 1)
    left = lane < SSD_HEAD_DIM
    y_parts = []
    for g in range(SSD_GROUPS):
        bg = xbc[:, SSD_INNER + g * SSD_STATE:SSD_INNER + (g + 1) * SSD_STATE]
        cg = xbc[:, SSD_INNER + SSD_GROUPS * SSD_STATE + g * SSD_STATE:
                 SSD_INNER + SSD_GROUPS * SSD_STATE + (g + 1) * SSD_STATE]
        bg16 = bg.astype(BF16)
        cg16 = cg.astype(BF16)
        cb = lax.dot_general(cg16, bg16, (((1,), (1,)), ((), ())), preferred_element_type=F32)
        for pair in range(SSD_HPG // 2):
            tile = g * (SSD_HPG // 2) + pair
            xp = xdt[:, tile * LANE:(tile + 1) * LANE]
            acc = None
            for side in range(2):
                j = 2 * tile + side
                seg = cum[:, j:j + 1] - cum_t[j:j + 1, :]
                decay = jnp.where(causal, jnp.exp(jnp.minimum(seg, 0.0)), 0.0)
                m = (cb * decay).astype(BF16)
                xsel = jnp.where(left if side == 0 else jnp.logical_not(left), xp, 0.0).astype(BF16)
                d = jnp.dot(m, xsel, preferred_element_type=F32)
                acc = d if acc is None else acc + d
            y_parts.append(acc)
    y_diag = jnp.concatenate(y_parts, axis=1)

    half = SSD_HPG * SSD_HEAD_DIM
    y_off_parts = []
    for g in range(SSD_GROUPS):
        bg = xbc[:, SSD_INNER + g * SSD_STATE:SSD_INNER + (g + 1) * SSD_STATE]
        cg = xbc[:, SSD_INNER + SSD_GROUPS * SSD_STATE + g * SSD_STATE:
                 SSD_INNER + SSD_GROUPS * SSD_STATE + (g + 1) * SSD_STATE]
        gs = slice(g * half, (g + 1) * half)
        prev = st_ref[:, gs]
        y_off_parts.append(jnp.dot(cg.astype(BF16), prev.astype(BF16), preferred_element_type=F32))
        new_states = jnp.dot(bg.T.astype(BF16), xw[:, gs], preferred_element_type=F32)
        st_ref[:, gs] = prev * ecum_e[L - 1:L, gs] + new_states
    y_off = jnp.concatenate(y_off_parts, axis=1) * ecum_e

    y = y_diag + y_off + dexp_ref[...] * xs
    yz = y * _silu(z_ref[...].astype(F32))
    ms = jnp.mean(yz * yz, axis=-1, keepdims=True)
    o_ref[...] = (yz * lax.rsqrt(ms + NORM_EPS) * ng_ref[...]).astype(o_ref.dtype)


def _ssd_branch(proj, dt_raw, bsz, seq, cw, cb, dt_bias, a_log, d_skip, norm_g):
    t = proj.shape[0]
    L = SSD_CHUNK
    nc = seq // L
    pad16 = lambda v: jnp.zeros((1, LANE), F32).at[0, :SSD_HEADS].set(v.astype(F32))
    d_exp = jnp.repeat(d_skip.astype(F32), SSD_HEAD_DIM).reshape(1, SSD_INNER)
    e_exp = (jnp.arange(LANE)[:, None] == (jnp.arange(SSD_INNER)[None, :] // SSD_HEAD_DIM)).astype(BF16)
    full = lambda shape: pl.BlockSpec(shape, lambda b, c: (0,) * len(shape))
    return pl.pallas_call(
        _ssd_body,
        grid=(bsz, nc),
        in_specs=[pl.BlockSpec((L, SSD_INNER), lambda b, c: (b * nc + c, 2)),
                  pl.BlockSpec((L, SSD_CONV_DIM), lambda b, c: (b * nc + c, 2)),
                  pl.BlockSpec((L, LANE), lambda b, c: (b * nc + c, 0)),
                  full((4, SSD_CONV_DIM)), full((1, SSD_CONV_DIM)),
                  full((1, LANE)), full((1, LANE)), full((1, SSD_INNER)), full((1, SSD_INNER)),
                  full((LANE, SSD_INNER))],
        out_specs=pl.BlockSpec((L, SSD_INNER), lambda b, c: (b * nc + c, 0)),
        out_shape=jax.ShapeDtypeStruct((t, SSD_INNER), BF16),
        scratch_shapes=[pltpu.VMEM((SUBLANE + L, SSD_CONV_DIM), F32),
                        pltpu.VMEM((SSD_STATE, SSD_INNER), F32)],
        compiler_params=_cparams(("parallel", "arbitrary")),
        name="ssd",
    )(proj, proj, dt_raw, cw.astype(F32), cb.reshape(1, -1).astype(F32), pad16(dt_bias), pad16(a_log),
      d_exp, norm_g.reshape(1, -1).astype(F32), e_exp)


def _even_out_body(ya_ref, yb_ref, wa_ref, wb_ref, res_ref, o_ref):
    o_ref[...] = (res_ref[...]
                  + jnp.dot(ya_ref[...], wa_ref[...], preferred_element_type=F32)
                  + jnp.dot(yb_ref[...], wb_ref[...], preferred_element_type=F32))


def _even_out(ya, yb, w_out, res, tm):
    t, d = res.shape
    w16 = w_out.astype(BF16)
    row = lambda n: pl.BlockSpec((tm, n), lambda i: (i, 0))
    return pl.pallas_call(
        _even_out_body,
        grid=(t // tm,),
        in_specs=[row(LRU_WIDTH), row(SSD_INNER),
                  pl.BlockSpec((LRU_WIDTH, d), lambda i: (0, 0)),
                  pl.BlockSpec((SSD_INNER, d), lambda i: (0, 0)),
                  row(d)],
        out_specs=row(d),
        out_shape=jax.ShapeDtypeStruct((t, d), F32),
        compiler_params=_cparams(("parallel",)),
        name="even_out",
    )(ya, yb, w16[:LRU_WIDTH], w16[LRU_WIDTH:], res)


def _odd_body(bg_ref, cg_ref, v_ref, cw_ref, w_ref, res_ref, o_ref, pad_ref, *, ts):
    w = D_MODEL

    @pl.when(pl.program_id(1) == 0)
    def _():
        pad_ref[0:SUBLANE, :] = jnp.zeros((SUBLANE, w), F32)

    cv = cg_ref[...].astype(F32) * v_ref[...].astype(F32)
    pad_ref[SUBLANE:SUBLANE + ts, :] = cv
    conv = cw_ref[2:3, :] * cv
    for k in range(2):
        conv = conv + cw_ref[k:k + 1, :] * pad_ref[6 + k:6 + k + ts, :]
    pad_ref[0:SUBLANE, :] = cv[ts - SUBLANE:ts, :]
    u = (bg_ref[...].astype(F32) * conv).astype(BF16)
    o_ref[...] = res_ref[...] + jnp.dot(u, w_ref[...], preferred_element_type=F32)


def _odd_mix_out(proj, bsz, seq, cw, w_out, res, ts):
    t, d = res.shape
    nt = seq // ts
    col = lambda c: pl.BlockSpec((ts, d), lambda b, s: (b * nt + s, c))
    return pl.pallas_call(
        functools.partial(_odd_body, ts=ts),
        grid=(bsz, nt),
        in_specs=[col(0), col(1), col(2),
                  pl.BlockSpec((3, d), lambda b, s: (0, 0)),
                  pl.BlockSpec((d, d), lambda b, s: (0, 0)),
                  col(0)],
        out_specs=col(0),
        out_shape=jax.ShapeDtypeStruct((t, d), F32),
        scratch_shapes=[pltpu.VMEM((SUBLANE + ts, d), F32)],
        compiler_params=_cparams(("parallel", "arbitrary")),
        name="odd_mix_out",
    )(proj, proj, proj, cw.astype(F32), w_out.astype(BF16), res)


def _topk_body(q_ref, sk_ref, r2_ref, e2_ref, n1_ref, c1_ref, rk_ref, sv_ref, nz_ref, s2_ref, bad_ref, *, tt):
    nl = tt // LANE
    iota_n = lax.broadcasted_iota(jnp.int32, (PEER_NKEYS, LANE), 0).astype(F32)

    def scores(h, p):
        qp = q_ref[:, (2 * h + p) * PEER_HALF:(2 * h + p + 1) * PEER_HALF]
        return lax.dot_general(sk_ref[h, p], qp, (((1,), (1,)), ((), ())),
                               preferred_element_type=F32)

    bad_ref[...] = jnp.zeros_like(bad_ref)
    tag_floor = -(2.0 ** 126)

    def fast_top16(h, p):
        x = scores(h, p)
        if p == 1:
            s2_ref[h] = x
        low = jnp.min(x, axis=0, keepdims=True)
        for r in range(PEER_TOPK):
            slabs = [x[i * SUBLANE:(i + 1) * SUBLANE, :] for i in range(PEER_NKEYS // SUBLANE)]
            while len(slabs) > 1:
                slabs = [jnp.maximum(slabs[i], slabs[i + 1]) for i in range(0, len(slabs), 2)]
            m = jnp.max(slabs[0], axis=0, keepdims=True)
            sv_ref[p, r, h:h + 1, :] = m
            x = jnp.where(x == m, -(2.0 ** 121) * (64 - r), x)
        tagged = x <= tag_floor
        rk_ref[p, h] = jnp.where(tagged, 64.0 + x * (2.0 ** -121), NOT_TOP)
        cnt = jnp.sum(jnp.where(tagged, 1.0, 0.0), axis=0, keepdims=True)
        bad_ref[0:1, :] += jnp.abs(cnt - float(PEER_TOPK)) + jnp.where(low <= tag_floor, 1.0, 0.0)

    for h in range(PEER_HEADS):
        for p in range(2):
            fast_top16(h, p)

    @pl.when(jnp.max(bad_ref[0:1, :]) > 0.0)
    def _():
        def top16(x0, p, h, lsl):
            def rnd(r, carry):
                x, rank, rf = carry
                m = jnp.max(x, axis=0, keepdims=True)
                idx = jnp.min(jnp.where(x == m, iota_n, float(PEER_NKEYS)), axis=0, keepdims=True)
                hit = iota_n == idx
                sv_ref[p, r, h:h + 1, lsl] = m
                return jnp.where(hit, -jnp.inf, x), jnp.where(hit, rf, rank), rf + 1.0

            init = (x0, jnp.full((PEER_NKEYS, LANE), NOT_TOP, F32), jnp.zeros((1, LANE), F32))
            return lax.fori_loop(0, PEER_TOPK, rnd, init)[1]

        for h in range(PEER_HEADS):
            for p in range(2):
                s = scores(h, p)
                for lt in range(nl):
                    lsl = slice(lt * LANE, (lt + 1) * LANE)
                    rk_ref[p, h, :, lsl] = top16(s[:, lsl], p, h, lsl)

    rk1_ref = rk_ref.at[0]
    for h in range(PEER_HEADS):
        r2_ref[h] = rk_ref[1, h].astype(BF16)
        e2_ref[h] = jnp.exp(s2_ref[h] - sv_ref[1, 0, h:h + 1, :]).astype(BF16)

    for lt in range(nl):
        lsl = slice(lt * LANE, (lt + 1) * LANE)
        sv1 = [sv_ref[0, i, :, lsl] for i in range(PEER_TOPK)]
        sv2 = [sv_ref[1, j, :, lsl] for j in range(PEER_TOPK)]
        cand = {c: sv1[c[0]] + sv2[c[1]] for c in _CELLS}
        beat = {c: None for c in _CELLS}
        lost = {c: None for c in _CELLS}
        nfirst = {c: 0 for c in _CELLS}
        add = lambda a, b: b if a is None else a + b
        for ci, c in enumerate(_CELLS):
            for c2 in _CELLS[ci + 1:]:
                if c2[0] > c[0] and c2[1] < c[1]:
                    m = jnp.where(cand[c] >= cand[c2], 1.0, 0.0)
                    beat[c2] = add(beat[c2], m)
                    lost[c] = add(lost[c], m)
                    nfirst[c] += 1
        e1 = [jnp.exp(sv1[i] - sv1[0]) for i in range(PEER_TOPK)]
        e2 = [jnp.exp(sv2[j] - sv2[0]) for j in range(PEER_TOPK)]
        nsel = [None] * PEER_TOPK
        zsum = None
        for c in _CELLS:
            base = float((c[0] + 1) * (c[1] + 1) - 1 + nfirst[c])
            cnt = base
            if beat[c] is not None:
                cnt = cnt + beat[c]
            if lost[c] is not None:
                cnt = cnt - lost[c]
            if isinstance(cnt, float):
                sel = jnp.full((PEER_HEADS, LANE), 1.0 if cnt < PEER_TOPK else 0.0, F32)
            else:
                sel = jnp.where(cnt < float(PEER_TOPK), 1.0, 0.0)
            nsel[c[0]] = add(nsel[c[0]], sel)
            zsum = add(zsum, sel * (e1[c[0]] * e2[c[1]]))
        for i in range(PEER_TOPK):
            nz_ref[0, i, :, lsl] = nsel[i]
            nz_ref[1, i, :, lsl] = e1[i] / zsum

    for h in range(PEER_HEADS):
        for lt in range(nl):
            lsl = slice(lt * LANE, (lt + 1) * LANE)
            rank1 = rk1_ref[h, :, lsl]
            nrow = jnp.zeros((PEER_NKEYS, LANE), F32)
            crow = jnp.zeros((PEER_NKEYS, LANE), F32)
            for i in range(PEER_TOPK):
                hit = rank1 == float(i)
                nrow = jnp.where(hit, nz_ref[0, i, h:h + 1, lsl], nrow)
                crow = jnp.where(hit, nz_ref[1, i, h:h + 1, lsl], crow)
            n1_ref[h, :, lsl] = nrow
            c1_ref[h, :, lsl] = crow


def _peer_topk(q, sub_keys, tt):
    t = q.shape[0]
    hk = (PEER_HEADS, PEER_NKEYS, t)
    blk = pl.BlockSpec((PEER_HEADS, PEER_NKEYS, tt), lambda i: (0, 0, i))
    return pl.pallas_call(
        functools.partial(_topk_body, tt=tt),
        grid=(t // tt,),
        in_specs=[pl.BlockSpec((tt, 2 * PEER_HEADS * PEER_HALF), lambda i: (i, 0)),
                  pl.BlockSpec((PEER_HEADS, 2, PEER_NKEYS, PEER_HALF), lambda i: (0, 0, 0, 0))],
        out_specs=[blk, blk, blk, blk],
        out_shape=[jax.ShapeDtypeStruct(hk, BF16), jax.ShapeDtypeStruct(hk, BF16),
                   jax.ShapeDtypeStruct(hk, F32), jax.ShapeDtypeStruct(hk, F32)],
        scratch_shapes=[pltpu.VMEM((2, PEER_HEADS, PEER_NKEYS, tt), F32),
                        pltpu.VMEM((2, PEER_TOPK, PEER_HEADS, tt), F32),
                        pltpu.VMEM((2, PEER_TOPK, PEER_HEADS, tt), F32),
                        pltpu.VMEM((PEER_HEADS, PEER_NKEYS, tt), F32),
                        pltpu.VMEM((SUBLANE, tt), F32)],
        compiler_params=_cparams(("parallel",)),
        name="peer_topk",
    )(q, sub_keys.astype(BF16))


PACK = 16


def _peer_mix_body(xn_ref, u_ref, vt_ref, r2_ref, e2_ref, n1_ref, c1_ref, res_ref, o_ref,
                   acc_ref, act_ref, w_ref, nc_ref, *, tt, eb, c1rows, c3rows):
    k = pl.program_id(1)
    nl = tt // LANE
    rep = PEER_NKEYS // PACK

    @pl.when(k == 0)
    def _():
        acc_ref[...] = jnp.zeros_like(acc_ref)

    for c3 in range(eb // c3rows):
        for c1 in range(c3rows // c1rows):
            r0 = c3 * c3rows + c1 * c1rows
            act_ref[r0:r0 + c1rows, :] = lax.dot_general(
                u_ref[r0:r0 + c1rows, :], xn_ref[...], (((1,), (1,)), ((), ())),
                preferred_element_type=F32)
            for al in range(r0 // PEER_NKEYS, (r0 + c1rows) // PEER_NKEYS):
                rows = slice(al * PEER_NKEYS, (al + 1) * PEER_NKEYS)
                for h in range(PEER_HEADS):
                    nc_ref[0, h, al] = jnp.broadcast_to(n1_ref[h, al:al + 1, :], (PACK, tt)).astype(BF16)
                    nc_ref[1, h, al] = jnp.broadcast_to(c1_ref[h, al:al + 1, :], (PACK, tt)).astype(BF16)
                for lt in range(nl):
                    lsl = slice(lt * LANE, (lt + 1) * LANE)
                    gate = None
                    for h in range(PEER_HEADS):
                        nb = jnp.concatenate([nc_ref[0, h, al, :, lsl]] * rep, axis=0)
                        cb = jnp.concatenate([nc_ref[1, h, al, :, lsl]] * rep, axis=0)
                        sel = jnp.minimum(jnp.maximum(nb - r2_ref[h, :, lsl], 0.0), cb)
                        term = sel * e2_ref[h, :, lsl]
                        gate = term if gate is None else gate + term
                    w_ref[rows, lsl] = _gelu_tanh(act_ref[rows, lsl].astype(BF16)) * gate
        q0 = c3 * c3rows
        acc_ref[...] += jnp.dot(vt_ref[:, q0:q0 + c3rows], w_ref[q0:q0 + c3rows, :],
                                preferred_element_type=F32)

    @pl.when(k == pl.num_programs(1) - 1)
    def _():
        o_ref[...] = res_ref[...] + acc_ref[...].T


def _peer_mix(xn, u16, vt16, r2, e2, n1, c1, res, tt, eb, c1rows=512, c3rows=1024):
    t, d = res.shape
    nblk = u16.shape[0] // eb
    na = eb // PEER_NKEYS
    rank_blk = pl.BlockSpec((PEER_HEADS, PEER_NKEYS, tt), lambda i, k: (0, 0, i))
    row_blk = pl.BlockSpec((PEER_HEADS, na, tt), lambda i, k: (0, k, i))
    return pl.pallas_call(
        functools.partial(_peer_mix_body, tt=tt, eb=eb, c1rows=c1rows, c3rows=c3rows),
        grid=(t // tt, nblk),
        in_specs=[pl.BlockSpec((tt, d), lambda i, k: (i, 0)),
                  pl.BlockSpec((eb, d), lambda i, k: (k, 0)),
                  pl.BlockSpec((d, eb), lambda i, k: (0, k)),
                  rank_blk, rank_blk, row_blk, row_blk,
                  pl.BlockSpec((tt, d), lambda i, k: (i, 0))],
        out_specs=pl.BlockSpec((tt, d), lambda i, k: (i, 0)),
        out_shape=jax.ShapeDtypeStruct((t, d), F32),
        scratch_shapes=[pltpu.VMEM((d, tt), F32),
                        pltpu.VMEM((eb, tt), F32),
                        pltpu.VMEM((eb, tt), BF16),
                        pltpu.VMEM((2, PEER_HEADS, na, PACK, tt), BF16)],
        compiler_params=_cparams(("parallel", "arbitrary")),
        name="peer_mix",
    )(xn, u16, vt16, r2, e2, n1, c1, res)


def _norm_body(x_ref, g_ref, o_ref):
    x = x_ref[...]
    ms = jnp.mean(x * x, axis=-1, keepdims=True)
    o_ref[...] = x * lax.rsqrt(ms + NORM_EPS) * g_ref[...]


def _rmsnorm(x, g, tm):
    t, d = x.shape
    return pl.pallas_call(
        _norm_body,
        grid=(t // tm,),
        in_specs=[pl.BlockSpec((tm, d), lambda i: (i, 0)), pl.BlockSpec((1, d), lambda i: (0, 0))],
        out_specs=pl.BlockSpec((tm, d), lambda i: (i, 0)),
        out_shape=jax.ShapeDtypeStruct((t, d), F32),
        compiler_params=_cparams(("parallel",)),
        name="final_norm",
    )(x, g.reshape(1, d).astype(F32))


def _tiles(bsz, seq):
    t = bsz * seq
    pick = lambda n, opts: next(o for o in opts if n % o == 0)
    return dict(tm=pick(t, (256, 128)), ts=pick(seq, (512, 256, 128)),
                tk=pick(t, (256, 128)), tt=pick(t, (512, 256, 128)), eb=1024)


def kernel(x, even_norm_g, even_w_in, lru_conv_w, lru_conv_b, lru_gate_a_w, lru_gate_a_b, lru_gate_x_w,
           lru_gate_x_b, lru_lambda, ssd_conv_w, ssd_conv_b, ssd_dt_bias, ssd_a_log, ssd_d, ssd_norm_g,
           even_w_out, odd_norm_g, odd_w_in, odd_conv_w, odd_w_out, ffn_norm_g, peer_w_query,
           peer_sub_keys, peer_u, peer_v, final_norm_g):
    bsz, seq, d = x.shape
    assert d == D_MODEL and seq % SSD_CHUNK == 0
    depth = ffn_norm_g.shape[0]
    cfg = _tiles(bsz, seq)
    h = x.reshape(bsz * seq, d)
    for layer in range(depth):
        i = layer // 2
        if layer % 2 == 0:
            w_in = jnp.pad(even_w_in[i].astype(BF16), ((0, 0), (0, EVEN_PAD - even_w_in.shape[2])))
            proj, dt_raw = _norm_matmul(h, even_norm_g[i], w_in, BF16, cfg["tm"], tail=EVEN_PAD - EVEN_MAIN)
            ya = _lru_branch(proj, bsz, seq, lru_conv_w[i], lru_conv_b[i], lru_gate_a_w[i], lru_gate_a_b[i],
                             lru_gate_x_w[i], lru_gate_x_b[i], lru_lambda[i], cfg["ts"])
            yb = _ssd_branch(proj, dt_raw, bsz, seq, ssd_conv_w[i], ssd_conv_b[i], ssd_dt_bias[i], ssd_a_log[i],
                             ssd_d[i], ssd_norm_g[i])
            h = _even_out(ya, yb, even_w_out[i], h, cfg["tm"])
        else:
            proj = _norm_matmul(h, odd_norm_g[i], odd_w_in[i].astype(BF16), BF16, cfg["tm"])
            h = _odd_mix_out(proj, bsz, seq, odd_conv_w[i], odd_w_out[i], h, cfg["ts"])
        q, xn = _norm_matmul(h, ffn_norm_g[layer], peer_w_query[layer].astype(BF16), BF16, cfg["tm"],
                             want_xn=True)
        r2, e2, n1, c1 = _peer_topk(q, peer_sub_keys[layer], cfg["tk"])
        h = _peer_mix(xn, peer_u[layer].astype(BF16), peer_v[layer].astype(BF16).T, r2, e2, n1, c1, h,
                      cfg["tt"], cfg["eb"])
    return _rmsnorm(h, final_norm_g, cfg["tm"]).reshape(bsz, seq, d)
```

```python
import functools

import jax
import jax.numpy as jnp
from jax import lax
from jax.experimental import pallas as pl
from jax.experimental.pallas import tpu as pltpu

F32 = jnp.float32
BF16 = jnp.bfloat16

D_MODEL = 1024
NORM_EPS = 1e-6

LRU_WIDTH = 1024
LRU_HEADS = 8
LRU_HEAD_DIM = 128
LRU_C = 8.0

SSD_INNER = 1024
SSD_HEAD_DIM = 64
SSD_HEADS = 16
SSD_GROUPS = 2
SSD_HPG = 8
SSD_STATE = 128
SSD_CHUNK = 128
SSD_CONV_DIM = SSD_INNER + 2 * SSD_GROUPS * SSD_STATE

EVEN_MAIN = 2 * LRU_WIDTH + SSD_INNER + SSD_CONV_DIM
EVEN_PAD = EVEN_MAIN + 128

PEER_HEADS = 8
PEER_NKEYS = 128
PEER_NEXPERTS = PEER_NKEYS * PEER_NKEYS
PEER_TOPK = 16
PEER_HALF = 128

LANE = 128
SUBLANE = 8
VMEM_LIMIT = 56 * 1024 * 1024

NOT_TOP = 100.0

_CELLS = [(i, j) for i in range(PEER_TOPK) for j in range(PEER_TOPK) if (i + 1) * (j + 1) <= PEER_TOPK]


def _cparams(sem):
    return pltpu.CompilerParams(dimension_semantics=sem, vmem_limit_bytes=VMEM_LIMIT)


def _split_bf16(x, terms):
    parts = []
    rem = x
    for _ in range(terms):
        p = rem.astype(BF16)
        parts.append(p)
        rem = rem - p.astype(F32)
    return parts


def _dot_exact_rhs01(x, m01, terms):
    acc = None
    for p in _split_bf16(x, terms):
        d = jnp.dot(p, m01, preferred_element_type=F32)
        acc = d if acc is None else acc + d
    return acc


def _dot_exact_lhs01(m01, x, terms):
    acc = None
    for p in _split_bf16(x, terms):
        d = jnp.dot(m01, p, preferred_element_type=F32)
        acc = d if acc is None else acc + d
    return acc


def _softplus(x):
    return jnp.maximum(x, 0.0) + jnp.log1p(jnp.exp(-jnp.abs(x)))


def _expm1(x):
    u = jnp.exp(x)
    um1 = u - 1.0
    return jnp.where(um1 == 0.0, x, um1 * x / jnp.log(jnp.where(um1 == 0.0, 2.0, u)))


def _sigmoid(x):
    return 0.5 + 0.5 * jnp.tanh(0.5 * x)


def _silu(x):
    return x * _sigmoid(x)


def _gelu_tanh(x):
    c = 0.7978845608028654
    hx = 0.5 * x
    return hx + hx * jnp.tanh(x * (c + (c * 0.044715) * (x * x)))


def _nmm_body(x_ref, g_ref, w_ref, o_ref, *extra, tail, want_xn):
    x = x_ref[...]
    ms = jnp.mean(x * x, axis=-1, keepdims=True)
    xn = (x * lax.rsqrt(ms + NORM_EPS) * g_ref[...]).astype(BF16)
    y = jnp.dot(xn, w_ref[...], preferred_element_type=F32)
    n = y.shape[1] - tail
    o_ref[...] = y[:, :n].astype(o_ref.dtype)
    extra = list(extra)
    if tail:
        extra.pop(0)[...] = y[:, n:]
    if want_xn:
        extra.pop(0)[...] = xn


def _norm_matmul(x, g, w_bf16, out_dtype, tm, want_xn=False, tail=0):
    t, d = x.shape
    n = w_bf16.shape[1]
    out_shape = [jax.ShapeDtypeStruct((t, n - tail), out_dtype)]
    out_specs = [pl.BlockSpec((tm, n - tail), lambda i: (i, 0))]
    if tail:
        out_shape.append(jax.ShapeDtypeStruct((t, tail), F32))
        out_specs.append(pl.BlockSpec((tm, tail), lambda i: (i, 0)))
    if want_xn:
        out_shape.append(jax.ShapeDtypeStruct((t, d), BF16))
        out_specs.append(pl.BlockSpec((tm, d), lambda i: (i, 0)))
    res = pl.pallas_call(
        functools.partial(_nmm_body, tail=tail, want_xn=want_xn),
        grid=(t // tm,),
        in_specs=[pl.BlockSpec((tm, d), lambda i: (i, 0)),
                  pl.BlockSpec((1, d), lambda i: (0, 0)),
                  pl.BlockSpec((d, n), lambda i: (0, 0))],
        out_specs=out_specs,
        out_shape=out_shape,
        compiler_params=_cparams(("parallel",)),
        name="norm_matmul",
    )(x, g.reshape(1, d), w_bf16)
    return res if (want_xn or tail) else res[0]


def _lru_body(gate_ref, x_ref, cw_ref, cb_ref, gaw_ref, gab_ref, gxw_ref, gxb_ref, lam_ref,
              o_ref, xpad_ref, a_ref, b_ref, carry_ref, *, ts):
    w = LRU_WIDTH

    @pl.when(pl.program_id(1) == 0)
    def _():
        xpad_ref[0:SUBLANE, :] = jnp.zeros((SUBLANE, w), F32)
        carry_ref[...] = jnp.zeros((SUBLANE, w), F32)

    xpad_ref[SUBLANE:SUBLANE + ts, :] = x_ref[...].astype(F32)
    xa = cb_ref[...] + cw_ref[3:4, :] * xpad_ref[SUBLANE:SUBLANE + ts, :]
    for k in range(3):
        xa = xa + cw_ref[k:k + 1, :] * xpad_ref[5 + k:5 + k + ts, :]
    xpad_ref[0:SUBLANE, :] = xpad_ref[ts:ts + SUBLANE, :]

    sp = _softplus(-lam_ref[...])
    for hd in range(LRU_HEADS):
        sl = slice(hd * LRU_HEAD_DIM, (hd + 1) * LRU_HEAD_DIM)
        xh = xa[:, sl]
        xh16 = xh.astype(BF16)
        r = _sigmoid(jnp.dot(xh16, gaw_ref[hd], preferred_element_type=F32) + gab_ref[:, sl])
        ig = _sigmoid(jnp.dot(xh16, gxw_ref[hd], preferred_element_type=F32) + gxb_ref[:, sl])
        log_a = (-LRU_C) * r * sp[:, sl]
        a_ref[:, sl] = jnp.exp(log_a)
        mult = jnp.sqrt(jnp.maximum(-_expm1(2.0 * log_a), 0.0))
        b_ref[:, sl] = mult * (ig * xh)

    row = lax.broadcasted_iota(jnp.int32, (SUBLANE, w), 0)

    def step(i, carry):
        r0 = pl.multiple_of(i * SUBLANE, SUBLANE)
        a = a_ref[pl.ds(r0, SUBLANE), :]
        b = b_ref[pl.ds(r0, SUBLANE), :]
        for d in (1, 2, 4):
            keep = row >= d
            a_sh = jnp.where(keep, pltpu.roll(a, d, 0), 1.0)
            b_sh = jnp.where(keep, pltpu.roll(b, d, 0), 0.0)
            b = a * b_sh + b
            a = a * a_sh
        hcur = b + a * carry
        b_ref[pl.ds(r0, SUBLANE), :] = hcur
        return jnp.broadcast_to(hcur[SUBLANE - 1:SUBLANE, :], (SUBLANE, w))

    carry = lax.fori_loop(0, ts // SUBLANE, step, carry_ref[...], unroll=2)
    carry_ref[...] = carry
    o_ref[...] = (_gelu_tanh(gate_ref[...].astype(F32)) * b_ref[...]).astype(o_ref.dtype)


def _lru_branch(proj, bsz, seq, cw, cb, gaw, gab, gxw, gxb, lam, ts):
    t = proj.shape[0]
    w = LRU_WIDTH
    nt = seq // ts
    vec = lambda v: v.reshape(1, w).astype(F32)
    full = lambda shape: pl.BlockSpec(shape, lambda b, s: (0,) * len(shape))
    return pl.pallas_call(
        functools.partial(_lru_body, ts=ts),
        grid=(bsz, nt),
        in_specs=[pl.BlockSpec((ts, w), lambda b, s: (b * nt + s, 0)),
                  pl.BlockSpec((ts, w), lambda b, s: (b * nt + s, 1)),
                  full((4, w)), full((1, w)),
                  full((LRU_HEADS, LRU_HEAD_DIM, LRU_HEAD_DIM)), full((1, w)),
                  full((LRU_HEADS, LRU_HEAD_DIM, LRU_HEAD_DIM)), full((1, w)),
                  full((1, w))],
        out_specs=pl.BlockSpec((ts, w), lambda b, s: (b * nt + s, 0)),
        out_shape=jax.ShapeDtypeStruct((t, w), BF16),
        scratch_shapes=[pltpu.VMEM((SUBLANE + ts, w), F32),
                        pltpu.VMEM((ts, w), F32),
                        pltpu.VMEM((ts, w), F32),
                        pltpu.VMEM((SUBLANE, w), F32)],
        compiler_params=_cparams(("parallel", "arbitrary")),
        name="rg_lru",
    )(proj, proj, cw.astype(F32), vec(cb), gaw.astype(BF16), vec(gab), gxw.astype(BF16), vec(gxb), vec(lam))


def _ssd_body(z_ref, xbc_ref, dt_ref, cw_ref, cb_ref, dtb_ref, alog_ref, dexp_ref, ng_ref, eexp_ref,
              o_ref, pad_ref, st_ref):
    L = SSD_CHUNK

    @pl.when(pl.program_id(1) == 0)
    def _():
        pad_ref[0:SUBLANE, :] = jnp.zeros((SUBLANE, SSD_CONV_DIM), F32)
        st_ref[...] = jnp.zeros((SSD_STATE, SSD_INNER), F32)

    pad_ref[SUBLANE:SUBLANE + L, :] = xbc_ref[...].astype(F32)
    conv = cb_ref[...] + cw_ref[3:4, :] * pad_ref[SUBLANE:SUBLANE + L, :]
    for k in range(3):
        conv = conv + cw_ref[k:k + 1, :] * pad_ref[5 + k:5 + k + L, :]
    pad_ref[0:SUBLANE, :] = pad_ref[L:L + SUBLANE, :]
    xbc = _silu(conv)
    xs = xbc[:, :SSD_INNER]

    dt = _softplus(dt_ref[...] + dtb_ref[...])
    da = dt * (-jnp.exp(alog_ref[...]))
    rr = lax.broadcasted_iota(jnp.int32, (L, L), 0)
    cc = lax.broadcasted_iota(jnp.int32, (L, L), 1)
    causal = rr >= cc
    tril = jnp.where(causal, 1.0, 0.0).astype(BF16)
    cum = _dot_exact_lhs01(tril, da, 3)
    cum_t = cum.T
    cum_last = cum[L - 1:L, :]
    w_end = jnp.exp(cum_last - cum) * dt
    ecum = jnp.exp(cum)
    stacked = jnp.concatenate([dt, w_end, ecum], axis=0)
    expd = _dot_exact_rhs01(stacked, eexp_ref[...], 2)
    dt_e = expd[0:L]
    wend_e = expd[L:2 * L]
    ecum_e = expd[2 * L:3 * L]
    xdt = xs * dt_e
    xw = (xs * wend_e).astype(BF16)

    lane = lax.broadcasted_iota(jnp.int32, (L, LANE), 1)
    left = lane < SSD_HEAD_DIM
    y_parts = []
    for g in range(SSD_GROUPS):
        bg = xbc[:, SSD_INNER + g * SSD_STATE:SSD_INNER + (g + 1) * SSD_STATE]
        cg = xbc[:, SSD_INNER + SSD_GROUPS * SSD_STATE + g * SSD_STATE:
                 SSD_INNER + SSD_GROUPS * SSD_STATE + (g + 1) * SSD_STATE]
        bg16 = bg.astype(BF16)
        cg16 = cg.astype(BF16)
        cb = lax.dot_general(cg16, bg16, (((1,), (1,)), ((), ())), preferred_element_type=F32)
        for pair in range(SSD_HPG // 2):
            tile = g * (SSD_HPG // 2) + pair
            xp = xdt[:, tile * LANE:(tile + 1) * LANE]
            acc = None
            for side in range(2):
                j = 2 * tile + side
                seg = cum[:, j:j + 1] - cum_t[j:j + 1, :]
                decay = jnp.where(causal, jnp.exp(jnp.minimum(seg, 0.0)), 0.0)
                m = (cb * decay).astype(BF16)
                xsel = jnp.where(left if side == 0 else jnp.logical_not(left), xp, 0.0).astype(BF16)
                d = jnp.dot(m, xsel, preferred_element_type=F32)
                acc = d if acc is None else acc + d
            y_parts.append(acc)
    y_diag = jnp.concatenate(y_parts, axis=1)

    half = SSD_HPG * SSD_HEAD_DIM
    y_off_parts = []
    for g in range(SSD_GROUPS):
        bg = xbc[:, SSD_INNER + g * SSD_STATE:SSD_INNER + (g + 1) * SSD_STATE]
        cg = xbc[:, SSD_INNER + SSD_GROUPS * SSD_STATE + g * SSD_STATE:
                 SSD_INNER + SSD_GROUPS * SSD_STATE + (g + 1) * SSD_STATE]
        gs = slice(g * half, (g + 1) * half)
        prev = st_ref[:, gs]
        y_off_parts.append(jnp.dot(cg.astype(BF16), prev.astype(BF16), preferred_element_type=F32))
        new_states = jnp.dot(bg.T.astype(BF16), xw[:, gs], preferred_element_type=F32)
        st_ref[:, gs] = prev * ecum_e[L - 1:L, gs] + new_states
    y_off = jnp.concatenate(y_off_parts, axis=1) * ecum_e

    y = y_diag + y_off + dexp_ref[...] * xs
    yz = y * _silu(z_ref[...].astype(F32))
    ms = jnp.mean(yz * yz, axis=-1, keepdims=True)
    o_ref[...] = (yz * lax.rsqrt(ms + NORM_EPS) * ng_ref[...]).astype(o_ref.dtype)


def _ssd_branch(proj, dt_raw, bsz, seq, cw, cb, dt_bias, a_log, d_skip, norm_g):
    t = proj.shape[0]
    L = SSD_CHUNK
    nc = seq // L
    pad16 = lambda v: jnp.zeros((1, LANE), F32).at[0, :SSD_HEADS].set(v.astype(F32))
    d_exp = jnp.repeat(d_skip.astype(F32), SSD_HEAD_DIM).reshape(1, SSD_INNER)
    e_exp = (jnp.arange(LANE)[:, None] == (jnp.arange(SSD_INNER)[None, :] // SSD_HEAD_DIM)).astype(BF16)
    full = lambda shape: pl.BlockSpec(shape, lambda b, c: (0,) * len(shape))
    return pl.pallas_call(
        _ssd_body,
        grid=(bsz, nc),
        in_specs=[pl.BlockSpec((L, SSD_INNER), lambda b, c: (b * nc + c, 2)),
                  pl.BlockSpec((L, SSD_CONV_DIM), lambda b, c: (b * nc + c, 2)),
                  pl.BlockSpec((L, LANE), lambda b, c: (b * nc + c, 0)),
                  full((4, SSD_CONV_DIM)), full((1, SSD_CONV_DIM)),
                  full((1, LANE)), full((1, LANE)), full((1, SSD_INNER)), full((1, SSD_INNER)),
                  full((LANE, SSD_INNER))],
        out_specs=pl.BlockSpec((L, SSD_INNER), lambda b, c: (b * nc + c, 0)),
        out_shape=jax.ShapeDtypeStruct((t, SSD_INNER), BF16),
        scratch_shapes=[pltpu.VMEM((SUBLANE + L, SSD_CONV_DIM), F32),
                        pltpu.VMEM((SSD_STATE, SSD_INNER), F32)],
        compiler_params=_cparams(("parallel", "arbitrary")),
        name="ssd",
    )(proj, proj, dt_raw, cw.astype(F32), cb.reshape(1, -1).astype(F32), pad16(dt_bias), pad16(a_log),
      d_exp, norm_g.reshape(1, -1).astype(F32), e_exp)


def _even_out_body(ya_ref, yb_ref, wa_ref, wb_ref, res_ref, o_ref):
    o_ref[...] = (res_ref[...]
                  + jnp.dot(ya_ref[...], wa_ref[...], preferred_element_type=F32)
                  + jnp.dot(yb_ref[...], wb_ref[...], preferred_element_type=F32))


def _even_out(ya, yb, w_out, res, tm):
    t, d = res.shape
    w16 = w_out.astype(BF16)
    row = lambda n: pl.BlockSpec((tm, n), lambda i: (i, 0))
    return pl.pallas_call(
        _even_out_body,
        grid=(t // tm,),
        in_specs=[row(LRU_WIDTH), row(SSD_INNER),
                  pl.BlockSpec((LRU_WIDTH, d), lambda i: (0, 0)),
                  pl.BlockSpec((SSD_INNER, d), lambda i: (0, 0)),
                  row(d)],
        out_specs=row(d),
        out_shape=jax.ShapeDtypeStruct((t, d), F32),
        compiler_params=_cparams(("parallel",)),
        name="even_out",
    )(ya, yb, w16[:LRU_WIDTH], w16[LRU_WIDTH:], res)


def _odd_body(bg_ref, cg_ref, v_ref, cw_ref, w_ref, res_ref, o_ref, pad_ref, *, ts):
    w = D_MODEL

    @pl.when(pl.program_id(1) == 0)
    def _():
        pad_ref[0:SUBLANE, :] = jnp.zeros((SUBLANE, w), F32)

    cv = cg_ref[...].astype(F32) * v_ref[...].astype(F32)
    pad_ref[SUBLANE:SUBLANE + ts, :] = cv
    conv = cw_ref[2:3, :] * cv
    for k in range(2):
        conv = conv + cw_ref[k:k + 1, :] * pad_ref[6 + k:6 + k + ts, :]
    pad_ref[0:SUBLANE, :] = cv[ts - SUBLANE:ts, :]
    u = (bg_ref[...].astype(F32) * conv).astype(BF16)
    o_ref[...] = res_ref[...] + jnp.dot(u, w_ref[...], preferred_element_type=F32)


def _odd_mix_out(proj, bsz, seq, cw, w_out, res, ts):
    t, d = res.shape
    nt = seq // ts
    col = lambda c: pl.BlockSpec((ts, d), lambda b, s: (b * nt + s, c))
    return pl.pallas_call(
        functools.partial(_odd_body, ts=ts),
        grid=(bsz, nt),
        in_specs=[col(0), col(1), col(2),
                  pl.BlockSpec((3, d), lambda b, s: (0, 0)),
                  pl.BlockSpec((d, d), lambda b, s: (0, 0)),
                  col(0)],
        out_specs=col(0),
        out_shape=jax.ShapeDtypeStruct((t, d), F32),
        scratch_shapes=[pltpu.VMEM((SUBLANE + ts, d), F32)],
        compiler_params=_cparams(("parallel", "arbitrary")),
        name="odd_mix_out",
    )(proj, proj, proj, cw.astype(F32), w_out.astype(BF16), res)


def _topk_body(q_ref, sk_ref, r2_ref, e2_ref, n1_ref, c1_ref, rk_ref, sv_ref, nz_ref, s2_ref, bad_ref, *, tt):
    nl = tt // LANE
    iota_n = lax.broadcasted_iota(jnp.int32, (PEER_NKEYS, LANE), 0).astype(F32)

    def scores(h, p):
        qp = q_ref[:, (2 * h + p) * PEER_HALF:(2 * h + p + 1) * PEER_HALF]
        return lax.dot_general(sk_ref[h, p], qp, (((1,), (1,)), ((), ())),
                               preferred_element_type=F32)

    bad_ref[...] = jnp.zeros_like(bad_ref)
    tag_floor = -(2.0 ** 126)

    def fast_top16(h, p):
        x = scores(h, p)
        if p == 1:
            s2_ref[h] = x
        low = jnp.min(x, axis=0, keepdims=True)
        for r in range(PEER_TOPK):
            slabs = [x[i * SUBLANE:(i + 1) * SUBLANE, :] for i in range(PEER_NKEYS // SUBLANE)]
            while len(slabs) > 1:
                slabs = [jnp.maximum(slabs[i], slabs[i + 1]) for i in range(0, len(slabs), 2)]
            m = jnp.max(slabs[0], axis=0, keepdims=True)
            sv_ref[p, r, h:h + 1, :] = m
            x = jnp.where(x == m, -(2.0 ** 121) * (64 - r), x)
        tagged = x <= tag_floor
        rk_ref[p, h] = jnp.where(tagged, 64.0 + x * (2.0 ** -121), NOT_TOP)
        cnt = jnp.sum(jnp.where(tagged, 1.0, 0.0), axis=0, keepdims=True)
        bad_ref[0:1, :] += jnp.abs(cnt - float(PEER_TOPK)) + jnp.where(low <= tag_floor, 1.0, 0.0)

    for h in range(PEER_HEADS):
        for p in range(2):
            fast_top16(h, p)

    @pl.when(jnp.max(bad_ref[0:1, :]) > 0.0)
    def _():
        def top16(x0, p, h, lsl):
            def rnd(r, carry):
                x, rank, rf = carry
                m = jnp.max(x, axis=0, keepdims=True)
                idx = jnp.min(jnp.where(x == m, iota_n, float(PEER_NKEYS)), axis=0, keepdims=True)
                hit = iota_n == idx
                sv_ref[p, r, h:h + 1, lsl] = m
                return jnp.where(hit, -jnp.inf, x), jnp.where(hit, rf, rank), rf + 1.0

            init = (x0, jnp.full((PEER_NKEYS, LANE), NOT_TOP, F32), jnp.zeros((1, LANE), F32))
            return lax.fori_loop(0, PEER_TOPK, rnd, init)[1]

        for h in range(PEER_HEADS):
            for p in range(2):
                s = scores(h, p)
                for lt in range(nl):
                    lsl = slice(lt * LANE, (lt + 1) * LANE)
                    rk_ref[p, h, :, lsl] = top16(s[:, lsl], p, h, lsl)

    rk1_ref = rk_ref.at[0]
    for h in range(PEER_HEADS):
        r2_ref[h] = rk_ref[1, h].astype(BF16)
        e2_ref[h] = jnp.exp(s2_ref[h] - sv_ref[1, 0, h:h + 1, :]).astype(BF16)

    for lt in range(nl):
        lsl = slice(lt * LANE, (lt + 1) * LANE)
        sv1 = [sv_ref[0, i, :, lsl] for i in range(PEER_TOPK)]
        sv2 = [sv_ref[1, j, :, lsl] for j in range(PEER_TOPK)]
        cand = {c: sv1[c[0]] + sv2[c[1]] for c in _CELLS}
        beat = {c: None for c in _CELLS}
        lost = {c: None for c in _CELLS}
        nfirst = {c: 0 for c in _CELLS}
        add = lambda a, b: b if a is None else a + b
        for ci, c in enumerate(_CELLS):
            for c2 in _CELLS[ci + 1:]:
                if c2[0] > c[0] and c2[1] < c[1]:
                    m = jnp.where(cand[c] >= cand[c2], 1.0, 0.0)
                    beat[c2] = add(beat[c2], m)
                    lost[c] = add(lost[c], m)
                    nfirst[c] += 1
        e1 = [jnp.exp(sv1[i] - sv1[0]) for i in range(PEER_TOPK)]
        e2 = [jnp.exp(sv2[j] - sv2[0]) for j in range(PEER_TOPK)]
        nsel = [None] * PEER_TOPK
        zsum = None
        for c in _CELLS:
            base = float((c[0] + 1) * (c[1] + 1) - 1 + nfirst[c])
            cnt = base
            if beat[c] is not None:
                cnt = cnt + beat[c]
            if lost[c] is not None:
                cnt = cnt - lost[c]
            if isinstance(cnt, float):
                sel = jnp.full((PEER_HEADS, LANE), 1.0 if cnt < PEER_TOPK else 0.0, F32)
            else:
                sel = jnp.where(cnt < float(PEER_TOPK), 1.0, 0.0)
            nsel[c[0]] = add(nsel[c[0]], sel)
            zsum = add(zsum, sel * (e1[c[0]] * e2[c[1]]))
        for i in range(PEER_TOPK):
            nz_ref[0, i, :, lsl] = nsel[i]
            nz_ref[1, i, :, lsl] = e1[i] / zsum

    for h in range(PEER_HEADS):
        for lt in range(nl):
            lsl = slice(lt * LANE, (lt + 1) * LANE)
            rank1 = rk1_ref[h, :, lsl]
            nrow = jnp.zeros((PEER_NKEYS, LANE), F32)
            crow = jnp.zeros((PEER_NKEYS, LANE), F32)
            for i in range(PEER_TOPK):
                hit = rank1 == float(i)
                nrow = jnp.where(hit, nz_ref[0, i, h:h + 1, lsl], nrow)
                crow = jnp.where(hit, nz_ref[1, i, h:h + 1, lsl], crow)
            n1_ref[h, :, lsl] = nrow
            c1_ref[h, :, lsl] = crow


def _peer_topk(q, sub_keys, tt):
    t = q.shape[0]
    hk = (PEER_HEADS, PEER_NKEYS, t)
    blk = pl.BlockSpec((PEER_HEADS, PEER_NKEYS, tt), lambda i: (0, 0, i))
    return pl.pallas_call(
        functools.partial(_topk_body, tt=tt),
        grid=(t // tt,),
        in_specs=[pl.BlockSpec((tt, 2 * PEER_HEADS * PEER_HALF), lambda i: (i, 0)),
                  pl.BlockSpec((PEER_HEADS, 2, PEER_NKEYS, PEER_HALF), lambda i: (0, 0, 0, 0))],
        out_specs=[blk, blk, blk, blk],
        out_shape=[jax.ShapeDtypeStruct(hk, BF16), jax.ShapeDtypeStruct(hk, BF16),
                   jax.ShapeDtypeStruct(hk, F32), jax.ShapeDtypeStruct(hk, F32)],
        scratch_shapes=[pltpu.VMEM((2, PEER_HEADS, PEER_NKEYS, tt), F32),
                        pltpu.VMEM((2, PEER_TOPK, PEER_HEADS, tt), F32),
                        pltpu.VMEM((2, PEER_TOPK, PEER_HEADS, tt), F32),
                        pltpu.VMEM((PEER_HEADS, PEER_NKEYS, tt), F32),
                        pltpu.VMEM((SUBLANE, tt), F32)],
        compiler_params=_cparams(("parallel",)),
        name="peer_topk",
    )(q, sub_keys.astype(BF16))


PACK = 16


def _peer_mix_body(xn_ref, u_ref, vt_ref, r2_ref, e2_ref, n1_ref, c1_ref, res_ref, o_ref,
                   acc_ref, act_ref, w_ref, nc_ref, *, tt, eb, c1rows, c3rows):
    k = pl.program_id(1)
    nl = tt // LANE
    rep = PEER_NKEYS // PACK

    @pl.when(k == 0)
    def _():
        acc_ref[...] = jnp.zeros_like(acc_ref)

    for c3 in range(eb // c3rows):
        for c1 in range(c3rows // c1rows):
            r0 = c3 * c3rows + c1 * c1rows
            act_ref[r0:r0 + c1rows, :] = lax.dot_general(
                u_ref[r0:r0 + c1rows, :], xn_ref[...], (((1,), (1,)), ((), ())),
                preferred_element_type=F32)
            for al in range(r0 // PEER_NKEYS, (r0 + c1rows) // PEER_NKEYS):
                rows = slice(al * PEER_NKEYS, (al + 1) * PEER_NKEYS)
                for h in range(PEER_HEADS):
                    nc_ref[0, h, al] = jnp.broadcast_to(n1_ref[h, al:al + 1, :], (PACK, tt)).astype(BF16)
                    nc_ref[1, h, al] = jnp.broadcast_to(c1_ref[h, al:al + 1, :], (PACK, tt)).astype(BF16)
                for lt in range(nl):
                    lsl = slice(lt * LANE, (lt + 1) * LANE)
                    gate = None
                    for h in range(PEER_HEADS):
                        nb = jnp.concatenate([nc_ref[0, h, al, :, lsl]] * rep, axis=0)
                        cb = jnp.concatenate([nc_ref[1, h, al, :, lsl]] * rep, axis=0)
                        sel = jnp.minimum(jnp.maximum(nb - r2_ref[h, :, lsl], 0.0), cb)
                        term = sel * e2_ref[h, :, lsl]
                        gate = term if gate is None else gate + term
                    w_ref[rows, lsl] = _gelu_tanh(act_ref[rows, lsl].astype(BF16)) * gate
        q0 = c3 * c3rows
        acc_ref[...] += jnp.dot(vt_ref[:, q0:q0 + c3rows], w_ref[q0:q0 + c3rows, :],
                                preferred_element_type=F32)

    @pl.when(k == pl.num_programs(1) - 1)
    def _():
        o_ref[...] = res_ref[...] + acc_ref[...].T


def _peer_mix(xn, u16, vt16, r2, e2, n1, c1, res, tt, eb, c1rows=512, c3rows=None):
    c3rows = eb if c3rows is None else c3rows
    t, d = res.shape
    nblk = u16.shape[0] // eb
    na = eb // PEER_NKEYS
    rank_blk = pl.BlockSpec((PEER_HEADS, PEER_NKEYS, tt), lambda i, k: (0, 0, i))
    row_blk = pl.BlockSpec((PEER_HEADS, na, tt), lambda i, k: (0, k, i))
    return pl.pallas_call(
        functools.partial(_peer_mix_body, tt=tt, eb=eb, c1rows=c1rows, c3rows=c3rows),
        grid=(t // tt, nblk),
        in_specs=[pl.BlockSpec((tt, d), lambda i, k: (i, 0)),
                  pl.BlockSpec((eb, d), lambda i, k: (k, 0)),
                  pl.BlockSpec((d, eb), lambda i, k: (0, k)),
                  rank_blk, rank_blk, row_blk, row_blk,
                  pl.BlockSpec((tt, d), lambda i, k: (i, 0))],
        out_specs=pl.BlockSpec((tt, d), lambda i, k: (i, 0)),
        out_shape=jax.ShapeDtypeStruct((t, d), F32),
        scratch_shapes=[pltpu.VMEM((d, tt), F32),
                        pltpu.VMEM((eb, tt), F32),
                        pltpu.VMEM((eb, tt), BF16),
                        pltpu.VMEM((2, PEER_HEADS, na, PACK, tt), BF16)],
        compiler_params=_cparams(("parallel", "arbitrary")),
        name="peer_mix",
    )(xn, u16, vt16, r2, e2, n1, c1, res)


def _norm_body(x_ref, g_ref, o_ref):
    x = x_ref[...]
    ms = jnp.mean(x * x, axis=-1, keepdims=True)
    o_ref[...] = x * lax.rsqrt(ms + NORM_EPS) * g_ref[...]


def _rmsnorm(x, g, tm):
    t, d = x.shape
    return pl.pallas_call(
        _norm_body,
        grid=(t // tm,),
        in_specs=[pl.BlockSpec((tm, d), lambda i: (i, 0)), pl.BlockSpec((1, d), lambda i: (0, 0))],
        out_specs=pl.BlockSpec((tm, d), lambda i: (i, 0)),
        out_shape=jax.ShapeDtypeStruct((t, d), F32),
        compiler_params=_cparams(("parallel",)),
        name="final_norm",
    )(x, g.reshape(1, d).astype(F32))


def _tiles(bsz, seq):
    t = bsz * seq
    pick = lambda n, opts: next(o for o in opts if n % o == 0)
    return dict(tm=pick(t, (256, 128)), ts=pick(seq, (512, 256, 128)),
                tk=pick(t, (256, 128)), tt=pick(t, (512, 256, 128)), eb=2048)


def kernel(x, even_norm_g, even_w_in, lru_conv_w, lru_conv_b, lru_gate_a_w, lru_gate_a_b, lru_gate_x_w,
           lru_gate_x_b, lru_lambda, ssd_conv_w, ssd_conv_b, ssd_dt_bias, ssd_a_log, ssd_d, ssd_norm_g,
           even_w_out, odd_norm_g, odd_w_in, odd_conv_w, odd_w_out, ffn_norm_g, peer_w_query,
           peer_sub_keys, peer_u, peer_v, final_norm_g):
    bsz, seq, d = x.shape
    assert d == D_MODEL and seq % SSD_CHUNK == 0
    depth = ffn_norm_g.shape[0]
    cfg = _tiles(bsz, seq)
    h = x.reshape(bsz * seq, d)
    for layer in range(depth):
        i = layer // 2
        if layer % 2 == 0:
            w_in = jnp.pad(even_w_in[i].astype(BF16), ((0, 0), (0, EVEN_PAD - even_w_in.shape[2])))
            proj, dt_raw = _norm_matmul(h, even_norm_g[i], w_in, BF16, cfg["tm"], tail=EVEN_PAD - EVEN_MAIN)
            ya = _lru_branch(proj, bsz, seq, lru_conv_w[i], lru_conv_b[i], lru_gate_a_w[i], lru_gate_a_b[i],
                             lru_gate_x_w[i], lru_gate_x_b[i], lru_lambda[i], cfg["ts"])
            yb = _ssd_branch(proj, dt_raw, bsz, seq, ssd_conv_w[i], ssd_conv_b[i], ssd_dt_bias[i], ssd_a_log[i],
                             ssd_d[i], ssd_norm_g[i])
            h = _even_out(ya, yb, even_w_out[i], h, cfg["tm"])
        else:
            proj = _norm_matmul(h, odd_norm_g[i], odd_w_in[i].astype(BF16), BF16, cfg["tm"])
            h = _odd_mix_out(proj, bsz, seq, odd_conv_w[i], odd_w_out[i], h, cfg["ts"])
        q, xn = _norm_matmul(h, ffn_norm_g[layer], peer_w_query[layer].astype(BF16), BF16, cfg["tm"],
                             want_xn=True)
        r2, e2, n1, c1 = _peer_topk(q, peer_sub_keys[layer], cfg["tk"])
        h = _peer_mix(xn, peer_u[layer].astype(BF16), peer_v[layer].astype(BF16).T, r2, e2, n1, c1, h,
                      cfg["tt"], cfg["eb"])
    return _rmsnorm(h, final_norm_g, cfg["tm"]).reshape(bsz, seq, d)
```
